```python
import math
import jax, jax.numpy as jnp
from jax import lax
import numpy as np

D_MODEL = 2048
BATCH = 8
SEQ = 4096
DEPTH = 1
DEC_BATCH = 4
DEC_SEQ = 8192
PAST_LEN = 128

GRID_W = 64
MIX_WIDTH = D_MODEL
HY_WIDTH = MIX_WIDTH // 2
NAT_WIDTH = MIX_WIDTH - HY_WIDTH
HY_ORDER = 2
HY_SHORT = 3
HY_EMB = 33
HY_FILTER_HIDDEN = 64
HY_FAST_DECAY = 0.3
HY_SLOW_DECAY = 1.5
HY_DECAY_TARGET = 1e-2
NAT_HEADS = 16
NAT_HEAD_DIM = NAT_WIDTH // NAT_HEADS
NAT_KH_MAX = 8
NAT_KW = 16
D_FF = 4 * D_MODEL
NORM_EPS = 1e-5

kernel_name = "hymba_hyena_natten_encoder"


def rms_norm(x, g):
    xf = x.astype(jnp.float32)
    y = xf * lax.rsqrt(jnp.mean(xf * xf, axis=-1, keepdims=True) + NORM_EPS)
    return (y * g.astype(jnp.float32)).astype(x.dtype)


def short_conv(u, w, b):
    up = jnp.pad(u, ((0, 0), (1, 1), (0, 0)))
    return up[:, :-2] * w[0] + up[:, 1:-1] * w[1] + up[:, 2:] * w[2] + b


def hyena_filters(L, w1, b1, w2, b2, w3, b3, freq, w4):
    f32 = jnp.float32
    t = jnp.linspace(0.0, 1.0, L, dtype=f32)[:, None]
    bands = (HY_EMB - 1) // 2
    w_ang = (2.0 * math.pi / L) * jnp.arange(L, dtype=f32)
    f = jnp.linspace(1e-4, bands - 1, bands, dtype=f32)
    ang = w_ang[:, None] * f[None, :]
    z = jnp.concatenate([t, jnp.cos(ang), -jnp.sin(ang)], axis=-1)
    fr = freq.astype(f32)
    h = jnp.sin(fr * (z @ w1.astype(f32) + b1.astype(f32)))
    h = jnp.sin(fr * (h @ w2.astype(f32) + b2.astype(f32)))
    h = jnp.sin(fr * (h @ w3.astype(f32) + b3.astype(f32)))
    h = (h @ w4.astype(f32)).reshape(L, HY_ORDER, 2, HY_WIDTH)
    max_decay = math.log(HY_DECAY_TARGET) / HY_FAST_DECAY
    min_decay = math.log(HY_DECAY_TARGET) / HY_SLOW_DECAY
    deltas = jnp.linspace(min_decay, max_decay, HY_WIDTH, dtype=f32)
    decay = jnp.exp(-t * jnp.abs(deltas)[None, :])
    h = h * decay[:, None, None, :]
    h_fwd = h[:, :, 0]
    h_bwd = h[:, :, 1]
    k = jnp.concatenate([h_fwd, jnp.zeros((1, HY_ORDER, HY_WIDTH), f32), h_bwd[:0:-1]], axis=0)
    return jnp.fft.rfft(k, axis=0)


def hyena_mixer(u, k_hat, bias):
    L = u.shape[1]
    v, x1, x2 = jnp.split(u.astype(jnp.float32), 3, axis=-1)
    bias = bias.astype(jnp.float32)
    z = v
    for o, gate in enumerate((x1, x2)):
        z_hat = jnp.fft.rfft(z, n=2 * L, axis=1)
        conv = jnp.fft.irfft(z_hat * k_hat[:, o][None], n=2 * L, axis=1)[:, :L]
        z = gate * (conv + z * bias[o])
    return z.astype(u.dtype)


def neighborhood_attention(q, k, v, rpb):
    B, L, H, hd = q.shape
    rows = L // GRID_W
    kh = min(NAT_KH_MAX, rows)
    qg = q.reshape(B, rows, GRID_W, H, hd)
    kg = k.reshape(B, rows, GRID_W, H, hd)
    vg = v.reshape(B, rows, GRID_W, H, hd)
    cols = jnp.arange(GRID_W)
    col_start = jnp.clip(cols - NAT_KW // 2, 0, GRID_W - NAT_KW)
    col_idx = col_start[:, None] + jnp.arange(NAT_KW)[None, :]
    col_off = col_idx - cols[:, None] + (NAT_KW - 1)
    scale = hd ** -0.5

    def row_block(i):
        rs = jnp.clip(i - kh // 2, 0, rows - kh)
        q_row = lax.dynamic_index_in_dim(qg, i, axis=1, keepdims=False)
        k_rows = lax.dynamic_slice_in_dim(kg, rs, kh, axis=1)
        v_rows = lax.dynamic_slice_in_dim(vg, rs, kh, axis=1)
        k_win = k_rows[:, :, col_idx]
        v_win = v_rows[:, :, col_idx]
        s = jnp.einsum('bjhd,bajkhd->bhjak', q_row, k_win).astype(jnp.float32) * scale
        row_off = rs + jnp.arange(kh) - i + (NAT_KH_MAX - 1)
        bias = rpb[:, row_off][:, :, col_off]
        s = s + jnp.transpose(bias, (0, 2, 1, 3)).astype(jnp.float32)[None]
        p = jax.nn.softmax(s.reshape(B, H, GRID_W, kh * NAT_KW), axis=-1)
        p = p.reshape(B, H, GRID_W, kh, NAT_KW).astype(v.dtype)
        return jnp.einsum('bhjak,bajkhd->bjhd', p, v_win)

    out = lax.map(row_block, jnp.arange(rows))
    return jnp.moveaxis(out, 0, 1).reshape(B, L, H * hd)


def encoder_trunk(x, norm_mix_g, w_in, hy_conv_w, hy_conv_b, hy_pe_w1, hy_pe_b1, hy_pe_w2,
                  hy_pe_b2, hy_pe_w3, hy_pe_b3, hy_pe_freq, hy_pe_w4, hy_bias, nat_rpb,
                  gnorm_hy, gnorm_nat, w_out, norm_mlp_g, w_up, w_down, norm_f_g):
    B, L, _ = x.shape
    h = x
    for l in range(DEPTH):
        k_hat = hyena_filters(L, hy_pe_w1[l], hy_pe_b1[l], hy_pe_w2[l], hy_pe_b2[l],
                              hy_pe_w3[l], hy_pe_b3[l], hy_pe_freq[l], hy_pe_w4[l])
        a = rms_norm(h, norm_mix_g[l])
        proj = a @ w_in[l]
        u_hy = short_conv(proj[..., :3 * HY_WIDTH], hy_conv_w[l], hy_conv_b[l])
        y_hy = hyena_mixer(u_hy, k_hat, hy_bias[l])
        q, k, v = jnp.split(proj[..., 3 * HY_WIDTH:], 3, axis=-1)
        q = q.reshape(B, L, NAT_HEADS, NAT_HEAD_DIM)
        k = k.reshape(B, L, NAT_HEADS, NAT_HEAD_DIM)
        v = v.reshape(B, L, NAT_HEADS, NAT_HEAD_DIM)
        y_nat = neighborhood_attention(q, k, v, nat_rpb[l])
        mix = jnp.concatenate([rms_norm(y_hy, gnorm_hy[l]), rms_norm(y_nat, gnorm_nat[l])], axis=-1)
        h = h + mix @ w_out[l]
        m = rms_norm(h, norm_mlp_g[l])
        h = h + jnp.square(jax.nn.relu(m @ w_up[l])) @ w_down[l]
    return rms_norm(h, norm_f_g)


def setup_inputs(seed: int = 0) -> dict:
    key = jax.random.key(seed)
    ks = jax.random.split(key, 24)
    f32 = jnp.float32
    nrm = lambda k, shape, s: jax.random.normal(k, shape, f32) * s
    H2 = HY_FILTER_HIDDEN
    return {
        "x_prompt": jax.random.normal(ks[0], (BATCH, SEQ, D_MODEL), f32),
        "x_sample": jax.random.normal(ks[1], (DEC_BATCH, DEC_SEQ, D_MODEL), f32),
        "norm_mix_g": 1.0 + nrm(ks[2], (DEPTH, D_MODEL), 0.02),
        "w_in": nrm(ks[3], (DEPTH, D_MODEL, 6 * HY_WIDTH), D_MODEL ** -0.5),
        "hy_conv_w": nrm(ks[4], (DEPTH, HY_SHORT, 3 * HY_WIDTH), HY_SHORT ** -0.5),
        "hy_conv_b": nrm(ks[5], (DEPTH, 3 * HY_WIDTH), 0.02),
        "hy_pe_w1": nrm(ks[6], (DEPTH, HY_EMB, H2), HY_EMB ** -0.5),
        "hy_pe_b1": nrm(ks[7], (DEPTH, H2), 0.02),
        "hy_pe_w2": nrm(ks[8], (DEPTH, H2, H2), H2 ** -0.5),
        "hy_pe_b2": nrm(ks[9], (DEPTH, H2), 0.02),
        "hy_pe_w3": nrm(ks[10], (DEPTH, H2, H2), H2 ** -0.5),
        "hy_pe_b3": nrm(ks[11], (DEPTH, H2), 0.02),
        "hy_pe_freq": 1.0 + nrm(ks[12], (DEPTH, H2), 0.01),
        "hy_pe_w4": nrm(ks[13], (DEPTH, H2, HY_ORDER * 2 * HY_WIDTH), 0.1 * H2 ** -0.5),
        "hy_bias": nrm(ks[14], (DEPTH, HY_ORDER, HY_WIDTH), 1.0),
        "nat_rpb": nrm(ks[15], (DEPTH, NAT_HEADS, 2 * NAT_KH_MAX - 1, 2 * NAT_KW - 1), 0.02),
        "gnorm_hy": 1.0 + nrm(ks[16], (DEPTH, HY_WIDTH), 0.02),
        "gnorm_nat": 1.0 + nrm(ks[17], (DEPTH, NAT_WIDTH), 0.02),
        "w_out": nrm(ks[18], (DEPTH, MIX_WIDTH, D_MODEL), MIX_WIDTH ** -0.5),
        "norm_mlp_g": 1.0 + nrm(ks[19], (DEPTH, D_MODEL), 0.02),
        "w_up": nrm(ks[20], (DEPTH, D_MODEL, D_FF), D_MODEL ** -0.5),
        "w_down": nrm(ks[21], (DEPTH, D_FF, D_MODEL), D_FF ** -0.5),
        "norm_f_g": 1.0 + nrm(ks[22], (D_MODEL,), 0.02),
    }


def reference(x_prompt, x_sample, norm_mix_g, w_in, hy_conv_w, hy_conv_b, hy_pe_w1, hy_pe_b1,
              hy_pe_w2, hy_pe_b2, hy_pe_w3, hy_pe_b3, hy_pe_freq, hy_pe_w4, hy_bias, nat_rpb,
              gnorm_hy, gnorm_nat, w_out, norm_mlp_g, w_up, w_down, norm_f_g):
    weights = (norm_mix_g, w_in, hy_conv_w, hy_conv_b, hy_pe_w1, hy_pe_b1, hy_pe_w2, hy_pe_b2,
               hy_pe_w3, hy_pe_b3, hy_pe_freq, hy_pe_w4, hy_bias, nat_rpb, gnorm_hy, gnorm_nat,
               w_out, norm_mlp_g, w_up, w_down, norm_f_g)
    y_prompt = encoder_trunk(x_prompt, *weights)
    y_sample = encoder_trunk(x_sample, *weights)
    return (y_prompt, y_sample)
```

```python
import functools
import math

import numpy as np
import jax
import jax.numpy as jnp
from jax import lax
from jax.experimental import pallas as pl
from jax.experimental.pallas import tpu as pltpu

F32 = jnp.float32
BF16 = jnp.bfloat16

LANES = 128
MXU_DIM = 256
VMEM_LIMIT = 56 * 1024 * 1024

NORM_EPS = 1e-5
GRID_W = 64
NAT_HEADS = 16
NAT_KH = 8
NAT_KW = 16
HY_ORDER = 2
HY_FAST_DECAY = 0.3
HY_SLOW_DECAY = 1.5
HY_DECAY_TARGET = 1e-2
MASK_VALUE = -1e30


def _cparams(sem):
    return pltpu.CompilerParams(dimension_semantics=sem, vmem_limit_bytes=VMEM_LIMIT)


def _rms(x, g):
    ms = jnp.mean(x * x, axis=-1, keepdims=True)
    return x * lax.rsqrt(ms + NORM_EPS) * g


def _in_proj_kernel(x_ref, g_ref, whT_ref, wn_ref, pT_ref, pn_ref, a_ref, *, n_hy):
    j = pl.program_id(1)

    @pl.when(j == 0)
    def _():
        a_ref[...] = _rms(x_ref[...], g_ref[...]).astype(BF16)

    @pl.when(j < n_hy)
    def _():
        pT_ref[0] = lax.dot_general(
            whT_ref[0], a_ref[...], (((1,), (1,)), ((), ())),
            preferred_element_type=F32).astype(BF16)

    @pl.when(j >= n_hy)
    def _():
        pn_ref[...] = jnp.dot(a_ref[...], wn_ref[0],
                              preferred_element_type=F32).astype(BF16)


def _in_proj(x2, g, whT, wn, batch, seq, tm):
    ntok, d = x2.shape
    n_hy, tc, _ = whT.shape
    n_nat = wn.shape[0]
    tpb = seq // tm
    return pl.pallas_call(
        functools.partial(_in_proj_kernel, n_hy=n_hy),
        grid=(ntok // tm, n_hy + n_nat),
        in_specs=[
            pl.BlockSpec((tm, d), lambda i, j: (i, 0)),
            pl.BlockSpec((1, d), lambda i, j: (0, 0)),
            pl.BlockSpec((1, tc, d), lambda i, j: (jnp.minimum(j, n_hy - 1), 0, 0)),
            pl.BlockSpec((1, d, tc), lambda i, j: (jnp.maximum(j - n_hy, 0), 0, 0)),
        ],
        out_specs=[
            pl.BlockSpec((1, tc, tm),
                         lambda i, j: (i // tpb, jnp.minimum(j, n_hy - 1), i % tpb)),
            pl.BlockSpec((tm, tc), lambda i, j: (i, jnp.maximum(j - n_hy, 0))),
        ],
        out_shape=[
            jax.ShapeDtypeStruct((batch, n_hy * tc, seq), BF16),
            jax.ShapeDtypeStruct((ntok, n_nat * tc), BF16),
        ],
        scratch_shapes=[pltpu.VMEM((tm, d), BF16)],
        compiler_params=_cparams(("arbitrary", "arbitrary")),
        name="in_proj",
    )(x2, g, whT, wn)


def _fft_constants(r):
    g = MXU_DIM // r
    n2f = 2 * r
    m = LANES * n2f
    k2 = np.arange(r, dtype=np.float64)[:, None]
    n2 = np.arange(r, dtype=np.float64)[None, :]
    th = 2.0 * np.pi * n2 * (k2 + 0.5) / n2f
    er, ei = np.cos(th), -np.sin(th)
    eye = np.eye(g)
    bd_fwd = np.concatenate([np.kron(eye, er), np.kron(eye, ei)], axis=0)
    bd_inv = np.concatenate([np.kron(eye, er.T), np.kron(eye, ei.T)], axis=1)
    bd_inv = bd_inv * (2.0 / m)
    n1 = np.arange(LANES, dtype=np.float64)[None, :]
    ph = 2.0 * np.pi * n1 * (k2 + 0.5) / m
    tw = np.stack([np.tile(np.cos(ph), (g, 1)), np.tile(-np.sin(ph), (g, 1))])
    a = np.arange(LANES, dtype=np.float64)
    ps = 2.0 * np.pi * np.outer(a, a) / LANES
    gr, gi = np.cos(ps), -np.sin(ps)
    g_fwd = np.block([[gr, gi], [-gi, gr]])
    g_inv = np.block([[gr, -gi], [gi, gr]])
    as_bf = lambda v: jnp.asarray(v, F32).astype(BF16)
    return as_bf(bd_fwd), as_bf(bd_inv), jnp.asarray(tw, F32), as_bf(g_fwd), as_bf(g_inv)


def _fft_fwd_tile(z, bd_fwd, tr, ti, g_fwd):
    a = jnp.dot(bd_fwd, z, preferred_element_type=F32)
    ar, ai = a[:MXU_DIM], a[MXU_DIM:]
    br = ar * tr - ai * ti
    bi = ar * ti + ai * tr
    b = jnp.concatenate([br, bi], axis=1).astype(BF16)
    return jnp.dot(b, g_fwd, preferred_element_type=F32)


def _fft_inv_tile(y, bd_inv, tr, ti, g_inv):
    p = jnp.dot(y.astype(BF16), g_inv, preferred_element_type=F32)
    pr, pi = p[:, :LANES], p[:, LANES:]
    qr = pr * tr + pi * ti
    qi = pi * tr - pr * ti
    q = jnp.concatenate([qr, qi], axis=0).astype(BF16)
    return jnp.dot(bd_inv, q, preferred_element_type=F32)


def _filter_mlp_kernel(zT_ref, t_ref, w1_ref, b1_ref, w2_ref, b2_ref, w3_ref, b3_ref,
                       fr_ref, w4_ref, ad_ref, o_ref, *, chunk):
    hp = lax.Precision.HIGHEST
    tl = zT_ref.shape[1]
    fr = fr_ref[...]
    h = jnp.sin(fr * (jnp.dot(w1_ref[...], zT_ref[...], precision=hp,
                              preferred_element_type=F32) + b1_ref[...]))
    h = jnp.sin(fr * (jnp.dot(w2_ref[...], h, precision=hp,
                              preferred_element_type=F32) + b2_ref[...]))
    h = jnp.sin(fr * (jnp.dot(w3_ref[...], h, precision=hp,
                              preferred_element_type=F32) + b3_ref[...]))
    t = t_ref[...]
    c = ad_ref.shape[0]
    first = (pl.program_id(0) == 0) & (lax.broadcasted_iota(jnp.int32, (chunk, tl), 1) == 0)
    for cc in range(c // chunk):
        ad = pltpu.repeat(ad_ref[cc * chunk:(cc + 1) * chunk, :], tl // LANES, axis=1)
        decay = jnp.exp(-(t * ad))
        for od in range(2 * HY_ORDER):
            r0 = od * c + cc * chunk
            v = jnp.dot(w4_ref[r0:r0 + chunk, :], h, precision=hp,
                        preferred_element_type=F32) * decay
            if od % 2 == 1:
                v = jnp.where(first, 0.0, v)
            o_ref[r0:r0 + chunk, :] = v


def _filter_mlp(zT, tvec, w1T, b1, w2T, b2, w3T, b3, fr, w4T, ad, tl):
    nz, seq = zT.shape
    rows, hid = w4T.shape
    c = ad.shape[0]
    full = lambda a: pl.BlockSpec(a.shape, lambda i: (0,) * a.ndim)
    return pl.pallas_call(
        functools.partial(_filter_mlp_kernel, chunk=512),
        grid=(seq // tl,),
        in_specs=[
            pl.BlockSpec((nz, tl), lambda i: (0, i)),
            pl.BlockSpec((1, tl), lambda i: (0, i)),
            full(w1T), full(b1), full(w2T), full(b2), full(w3T), full(b3), full(fr),
            full(w4T), full(ad),
        ],
        out_specs=pl.BlockSpec((rows, tl), lambda i: (0, i)),
        out_shape=jax.ShapeDtypeStruct((rows, seq), F32),
        compiler_params=_cparams(("arbitrary",)),
        name="filter_mlp",
    )(zT, tvec, w1T, b1, w2T, b2, w3T, b3, fr, w4T, ad)


def _filter_fft_kernel(hf_ref, hb_ref, bdf_ref, tw_ref, gf_ref, o_ref):
    nt = hf_ref.shape[0] // MXU_DIM

    def body(t, carry):
        r0 = pl.multiple_of(t * MXU_DIM, MXU_DIM)
        tr, ti = tw_ref[0], tw_ref[1]
        xf = _fft_fwd_tile(hf_ref[pl.ds(r0, MXU_DIM), :].astype(BF16), bdf_ref[...], tr, ti,
                           gf_ref[...])
        xb = _fft_fwd_tile(hb_ref[pl.ds(r0, MXU_DIM), :].astype(BF16), bdf_ref[...], tr, ti,
                           gf_ref[...])
        o_ref[0, pl.ds(r0, MXU_DIM), :LANES] = xf[:, :LANES] + xb[:, :LANES]
        o_ref[0, pl.ds(r0, MXU_DIM), LANES:] = xf[:, LANES:] - xb[:, LANES:]
        return carry

    lax.fori_loop(0, nt, body, 0)


def _filter_fft(hT2, consts, c, r, rb):
    bd_fwd, _, tw, g_fwd, _ = consts
    nblk = c * r // rb
    full = lambda a: pl.BlockSpec(a.shape, lambda o, j: (0,) * a.ndim)
    return pl.pallas_call(
        _filter_fft_kernel,
        grid=(HY_ORDER, nblk),
        in_specs=[
            pl.BlockSpec((rb, LANES), lambda o, j: ((2 * o) * nblk + j, 0)),
            pl.BlockSpec((rb, LANES), lambda o, j: ((2 * o + 1) * nblk + j, 0)),
            full(bd_fwd), full(tw), full(g_fwd),
        ],
        out_specs=pl.BlockSpec((1, rb, 2 * LANES), lambda o, j: (o, j, 0)),
        out_shape=jax.ShapeDtypeStruct((HY_ORDER, c * r, 2 * LANES), F32),
        compiler_params=_cparams(("arbitrary", "arbitrary")),
        name="filter_fft",
    )(hT2, hT2, bd_fwd, tw, g_fwd)


def _hyena_kernel(v_ref, x1_ref, x2_ref, par_ref, kh_ref, bdf_ref, bdi_ref, tw_ref, gf_ref,
                  gi_ref, o_ref, *, r):
    g = MXU_DIM // r
    nt = v_ref.shape[1] // MXU_DIM
    lane = lax.broadcasted_iota(jnp.int32, (MXU_DIM, LANES), 1)
    n2 = lax.broadcasted_iota(jnp.int32, (MXU_DIM, LANES), 0) % r
    lane_first = lane == 0
    lane_last = lane == LANES - 1
    seq_first = lane_first & (n2 == 0)
    seq_last = lane_last & (n2 == r - 1)

    def rowwise(par):
        return jnp.concatenate(
            [jnp.broadcast_to(par[c:c + 1, :], (r, LANES)) for c in range(g)], axis=0)

    def short_conv(p, w):
        a = pltpu.roll(p, 1, axis=1)
        prev = jnp.where(lane_first, pltpu.roll(a, 1, axis=0), a)
        prev = jnp.where(seq_first, 0.0, prev)
        b = pltpu.roll(p, LANES - 1, axis=1)
        nxt = jnp.where(lane_last, pltpu.roll(b, MXU_DIM - 1, axis=0), b)
        nxt = jnp.where(seq_last, 0.0, nxt)
        return prev * rowwise(w[0]) + p * rowwise(w[1]) + nxt * rowwise(w[2]) + rowwise(w[3])

    def body(t, carry):
        r0 = pl.multiple_of(t * MXU_DIM, MXU_DIM)
        par = par_ref[:, t]
        tr, ti = tw_ref[0], tw_ref[1]
        z = short_conv(v_ref[0, pl.ds(r0, MXU_DIM), :].astype(F32), par[0:4])
        gates = (x1_ref, x2_ref)
        for o in range(HY_ORDER):
            gate = short_conv(gates[o][0, pl.ds(r0, MXU_DIM), :].astype(F32),
                              par[4 + 4 * o:8 + 4 * o])
            x = _fft_fwd_tile(z.astype(BF16), bdf_ref[...], tr, ti, gf_ref[...])
            xr, xi = x[:, :LANES], x[:, LANES:]
            kr = kh_ref[o, pl.ds(r0, MXU_DIM), :LANES]
            ki = kh_ref[o, pl.ds(r0, MXU_DIM), LANES:]
            y = jnp.concatenate([xr * kr - xi * ki, xr * ki + xi * kr], axis=1)
            conv = _fft_inv_tile(y, bdi_ref[...], tr, ti, gi_ref[...])
            z = gate * (conv + z * rowwise(par[12 + o]))
        o_ref[0, pl.ds(r0, MXU_DIM), :] = z.astype(o_ref.dtype)
        return carry

    lax.fori_loop(0, nt, body, 0)


def _hyena(pT3, par, khat, consts, batch, c, r, rb):
    bd_fwd, bd_inv, tw, g_fwd, g_inv = consts
    g = MXU_DIM // r
    nblk = c * r // rb
    tiles = rb // MXU_DIM
    full = lambda a: pl.BlockSpec(a.shape, lambda j, b: (0,) * a.ndim)
    part = lambda k: pl.BlockSpec((1, rb, LANES), lambda j, b, k=k: (b, k * nblk + j, 0))
    return pl.pallas_call(
        functools.partial(_hyena_kernel, r=r),
        grid=(nblk, batch),
        in_specs=[
            part(0), part(1), part(2),
            pl.BlockSpec((par.shape[0], tiles, g, LANES), lambda j, b: (0, j, 0, 0)),
            pl.BlockSpec((HY_ORDER, rb, 2 * LANES), lambda j, b: (0, j, 0)),
            full(bd_fwd), full(bd_inv), full(tw), full(g_fwd), full(g_inv),
        ],
        out_specs=pl.BlockSpec((1, rb, LANES), lambda j, b: (b, j, 0)),
        out_shape=jax.ShapeDtypeStruct((batch, c * r, LANES), BF16),
        compiler_params=_cparams(("arbitrary", "arbitrary")),
        name="hyena",
    )(pT3, pT3, pT3, par, khat, bd_fwd, bd_inv, tw, g_fwd, g_inv)


def _nat_kernel(q_ref, k_ref, v_ref, bias_ref, o_ref, *, rows, scale):
    npair = q_ref.shape[2] // LANES
    win = NAT_KH * GRID_W
    lo = lax.broadcasted_iota(jnp.int32, (GRID_W, LANES), 1) < LANES // 2

    def body(i, carry):
        rs = jnp.clip(i - NAT_KH // 2, 0, rows - NAT_KH)
        d = rs - i + (NAT_KH - 1)
        q0 = pl.multiple_of(i * GRID_W, GRID_W)
        k0 = pl.multiple_of(rs * GRID_W, GRID_W)
        for p in range(npair):
            cols = slice(p * LANES, (p + 1) * LANES)
            qp = q_ref[0, pl.ds(q0, GRID_W), cols].astype(F32) * scale
            kp = k_ref[0, pl.ds(k0, win), cols]
            vp = v_ref[0, pl.ds(k0, win), cols]
            outs = []
            for e in range(2):
                qe = jnp.where(lo if e == 0 else jnp.logical_not(lo), qp, 0.0).astype(BF16)
                s = lax.dot_general(qe, kp, (((1,), (1,)), ((), ())),
                                    preferred_element_type=F32)
                s = s + bias_ref[d, 2 * p + e]
                m = jnp.max(s, axis=-1, keepdims=True)
                pe = jnp.exp(s - m)
                l = jnp.sum(pe, axis=-1, keepdims=True)
                o = jnp.dot(pe.astype(BF16), vp, preferred_element_type=F32)
                outs.append(o / l)
            o_ref[0, pl.ds(q0, GRID_W), cols] = jnp.where(lo, outs[0], outs[1]).astype(o_ref.dtype)
        return carry

    lax.fori_loop(0, rows, body, 0)


def _nat(pn3, bias, batch, seq, width, hd):
    gw = 2 * LANES
    ng = width // gw
    hpg = gw // hd
    rows = seq // GRID_W
    blk = lambda k: pl.BlockSpec((1, seq, gw), lambda b, hg, k=k: (b, 0, k * ng + hg))
    return pl.pallas_call(
        functools.partial(_nat_kernel, rows=rows, scale=hd ** -0.5),
        grid=(batch, ng),
        in_specs=[
            blk(0), blk(1), blk(2),
            pl.BlockSpec((NAT_KH, hpg, GRID_W, NAT_KH * GRID_W), lambda b, hg: (0, hg, 0, 0)),
        ],
        out_specs=pl.BlockSpec((1, seq, gw), lambda b, hg: (b, 0, hg)),
        out_shape=jax.ShapeDtypeStruct((batch, seq, width), BF16),
        compiler_params=_cparams(("arbitrary", "arbitrary")),
        name="nat",
    )(pn3, pn3, pn3, bias)


def _nat_bias_table(rpb):
    h = rpb.shape[0]
    cols = np.arange(GRID_W)
    col_start = np.clip(cols - NAT_KW // 2, 0, GRID_W - NAT_KW)
    cc = cols[None, :]
    valid = (cc >= col_start[:, None]) & (cc < col_start[:, None] + NAT_KW)
    col_off = np.clip(cc - cols[:, None] + (NAT_KW - 1), 0, 2 * NAT_KW - 2)
    row_off = np.arange(NAT_KH)[:, None] + np.arange(NAT_KH)[None, :]
    t = rpb.astype(F32)[:, row_off]
    t = t[:, :, :, col_off]
    t = jnp.where(valid[None, None, None], t, MASK_VALUE)
    t = jnp.transpose(t, (1, 0, 3, 2, 4))
    return t.reshape(NAT_KH, h, GRID_W, NAT_KH * GRID_W)


def _out_proj_kernel(x_ref, yh_ref, yn_ref, gh_ref, gn_ref, woh_ref, won_ref, h_ref):
    tm = x_ref.shape[0]
    yh = yh_ref[0].astype(F32)
    ms = jnp.mean(yh * yh, axis=0, keepdims=True)
    gh = pltpu.repeat(gh_ref[...], tm // LANES, axis=1)
    nh = (yh * lax.rsqrt(ms + NORM_EPS) * gh).astype(BF16)
    nn = _rms(yn_ref[...].astype(F32), gn_ref[...]).astype(BF16)
    acc = lax.dot_general(nh, woh_ref[...], (((0,), (0,)), ((), ())),
                          preferred_element_type=F32)
    acc = acc + jnp.dot(nn, won_ref[...], preferred_element_type=F32)
    h_ref[...] = x_ref[...] + acc


def _out_proj(x2, yhT, yn, gh, gn, woh, won, seq, tm):
    ntok, d = x2.shape
    c = yn.shape[1]
    tpb = seq // tm
    full = lambda a: pl.BlockSpec(a.shape, lambda i: (0,) * a.ndim)
    return pl.pallas_call(
        _out_proj_kernel,
        grid=(ntok // tm,),
        in_specs=[
            pl.BlockSpec((tm, d), lambda i: (i, 0)),
            pl.BlockSpec((1, c, tm), lambda i: (i // tpb, 0, i % tpb)),
            pl.BlockSpec((tm, c), lambda i: (i, 0)),
            full(gh), full(gn), full(woh), full(won),
        ],
        out_specs=pl.BlockSpec((tm, d), lambda i: (i, 0)),
        out_shape=jax.ShapeDtypeStruct((ntok, d), F32),
        compiler_params=_cparams(("arbitrary",)),
        name="out_proj",
    )(x2, yhT, yn, gh, gn, woh, won)


def _mlp_kernel(h_ref, g_ref, wu_ref, wd_ref, gf_ref, o_ref, a_ref):
    j = pl.program_id(1)

    @pl.when(j == 0)
    def _():
        h = h_ref[...]
        a_ref[...] = _rms(h, g_ref[...]).astype(BF16)
        o_ref[...] = h

    u = jnp.dot(a_ref[...], wu_ref[...], preferred_element_type=F32)
    u = jnp.square(jnp.maximum(u, 0.0)).astype(BF16)
    o_ref[...] += jnp.dot(u, wd_ref[...], preferred_element_type=F32)

    @pl.when(j == pl.num_programs(1) - 1)
    def _():
        o_ref[...] = _rms(o_ref[...], gf_ref[...])


def _mlp(h, g, wu, wd, gf, tm, tf):
    ntok, d = h.shape
    dff = wu.shape[1]
    return pl.pallas_call(
        _mlp_kernel,
        grid=(ntok // tm, dff // tf),
        in_specs=[
            pl.BlockSpec((tm, d), lambda i, j: (i, 0)),
            pl.BlockSpec((1, d), lambda i, j: (0, 0)),
            pl.BlockSpec((d, tf), lambda i, j: (0, j)),
            pl.BlockSpec((tf, d), lambda i, j: (j, 0)),
            pl.BlockSpec((1, d), lambda i, j: (0, 0)),
        ],
        out_specs=pl.BlockSpec((tm, d), lambda i, j: (i, 0)),
        out_shape=jax.ShapeDtypeStruct((ntok, d), F32),
        scratch_shapes=[pltpu.VMEM((tm, d), BF16)],
        compiler_params=_cparams(("arbitrary", "arbitrary")),
        name="mlp",
    )(h, g, wu, wd, gf)


def _positional_features(seq, emb):
    bands = (emb - 1) // 2
    t = np.linspace(0.0, 1.0, seq)
    w_ang = (2.0 * np.pi / seq) * np.arange(seq)
    f = np.linspace(1e-4, bands - 1, bands)
    ang = w_ang[None, :] * f[:, None]
    z = np.concatenate([t[None, :], np.cos(ang), -np.sin(ang)], axis=0)
    pad = (-z.shape[0]) % 8
    z = np.concatenate([z, np.zeros((pad, seq))], axis=0)
    return jnp.asarray(z, F32), jnp.asarray(t[None, :], F32)


def _lane_rep(v):
    return jnp.broadcast_to(v.astype(F32)[..., None], v.shape + (LANES,))


def _trunk(x, w):
    batch, seq, d = x.shape
    c = w["hy_bias"].shape[-1]
    r = seq // LANES
    g = MXU_DIM // r
    rows = seq // GRID_W
    assert seq % LANES == 0 and MXU_DIM % r == 0 and r % 8 == 0
    assert rows >= NAT_KH and seq % GRID_W == 0
    consts = _fft_constants(r)

    emb = w["hy_pe_w1"].shape[0]
    zT, tvec = _positional_features(seq, emb)
    w1T = jnp.pad(w["hy_pe_w1"].astype(F32).T, ((0, 0), (0, zT.shape[0] - emb)))
    col = lambda v: v.astype(F32)[:, None]
    max_decay = math.log(HY_DECAY_TARGET) / HY_FAST_DECAY
    min_decay = math.log(HY_DECAY_TARGET) / HY_SLOW_DECAY
    absd = jnp.asarray(np.abs(np.linspace(min_decay, max_decay, c)), F32)
    hT = _filter_mlp(zT, tvec, w1T, col(w["hy_pe_b1"]), w["hy_pe_w2"].astype(F32).T,
                     col(w["hy_pe_b2"]), w["hy_pe_w3"].astype(F32).T, col(w["hy_pe_b3"]),
                     col(w["hy_pe_freq"]), w["hy_pe_w4"].astype(F32).T, _lane_rep(absd),
                     tl=512)
    rb = 2048
    khat = _filter_fft(hT.reshape(-1, LANES), consts, c, r, rb)

    x2 = x.reshape(batch * seq, d)
    w_in = w["w_in"]
    n_hy = 3 * c
    tc = 1024
    whT = w_in[:, :n_hy].T.astype(BF16).reshape(n_hy // tc, tc, d)
    wn = w_in[:, n_hy:].astype(BF16).reshape(d, -1, tc).transpose(1, 0, 2)
    pT, pn = _in_proj(x2, w["norm_mix_g"].astype(F32)[None, :], whT, wn, batch, seq, tm=1024)

    cw = w["hy_conv_w"].astype(F32).reshape(3, 3, c)
    cb = w["hy_conv_b"].astype(F32).reshape(3, c)
    par = [jnp.concatenate([cw[:, p], cb[p][None]], axis=0) for p in range(3)]
    par = jnp.concatenate(par + [w["hy_bias"].astype(F32)], axis=0)
    par = _lane_rep(par).reshape(par.shape[0], c // g, g, LANES)
    yhT = _hyena(pT.reshape(batch, n_hy * r, LANES), par, khat, consts, batch, c, r, rb)
    yhT = yhT.reshape(batch, c, seq)

    nat_w = pn.shape[1] // 3
    bias = _nat_bias_table(w["nat_rpb"])
    yn = _nat(pn.reshape(batch, seq, 3 * nat_w), bias, batch, seq, nat_w, nat_w // NAT_HEADS)
    yn = yn.reshape(batch * seq, nat_w)

    w_out = w["w_out"].astype(BF16)
    h = _out_proj(x2, yhT, yn, _lane_rep(w["gnorm_hy"]), w["gnorm_nat"].astype(F32)[None, :],
                  w_out[:c], w_out[c:], seq, tm=512)
    y = _mlp(h, w["norm_mlp_g"].astype(F32)[None, :], w["w_up"].astype(BF16),
             w["w_down"].astype(BF16), w["norm_f_g"].astype(F32)[None, :], tm=512, tf=1024)
    return y.reshape(batch, seq, d)


def kernel(x_prompt, x_sample, norm_mix_g, w_in, hy_conv_w, hy_conv_b, hy_pe_w1, hy_pe_b1,
           hy_pe_w2, hy_pe_b2, hy_pe_w3, hy_pe_b3, hy_pe_freq, hy_pe_w4, hy_bias, nat_rpb,
           gnorm_hy, gnorm_nat, w_out, norm_mlp_g, w_up, w_down, norm_f_g):
    assert norm_mix_g.shape[0] == 1, "single-layer trunk"
    w = dict(norm_mix_g=norm_mix_g[0], w_in=w_in[0], hy_conv_w=hy_conv_w[0],
             hy_conv_b=hy_conv_b[0], hy_pe_w1=hy_pe_w1[0], hy_pe_b1=hy_pe_b1[0],
             hy_pe_w2=hy_pe_w2[0], hy_pe_b2=hy_pe_b2[0], hy_pe_w3=hy_pe_w3[0],
             hy_pe_b3=hy_pe_b3[0], hy_pe_freq=hy_pe_freq[0], hy_pe_w4=hy_pe_w4[0],
             hy_bias=hy_bias[0], nat_rpb=nat_rpb[0], gnorm_hy=gnorm_hy[0],
             gnorm_nat=gnorm_nat[0], w_out=w_out[0], norm_mlp_g=norm_mlp_g[0],
             w_up=w_up[0], w_down=w_down[0], norm_f_g=norm_f_g)
    return (_trunk(x_prompt, w), _trunk(x_sample, w))
```

```python
import functools
import math

import numpy as np
import jax
import jax.numpy as jnp
from jax import lax
from jax.experimental import pallas as pl
from jax.experimental.pallas import tpu as pltpu

F32 = jnp.float32
BF16 = jnp.bfloat16

LANES = 128
SUBLANES = 8
STRIP = 2 * SUBLANES
VMEM_LIMIT = 56 * 1024 * 1024

NORM_EPS = 1e-5
GRID_W = 64
NAT_HEADS = 16
NAT_KH = 8
NAT_KW = 16
HY_ORDER = 2
HY_FAST_DECAY = 0.3
HY_SLOW_DECAY = 1.5
HY_DECAY_TARGET = 1e-2
MASK_VALUE = -1e30


def _cparams(sem):
    return pltpu.CompilerParams(dimension_semantics=sem, vmem_limit_bytes=VMEM_LIMIT)


def _rms(x, g):
    ms = jnp.mean(x * x, axis=-1, keepdims=True)
    return x * lax.rsqrt(ms + NORM_EPS) * g


def _in_proj_kernel(x_ref, g_ref, whT_ref, wn_ref, pT_ref, pn_ref, a_ref, *, n_hy):
    j = pl.program_id(1)

    @pl.when(j == 0)
    def _():
        a_ref[...] = _rms(x_ref[...], g_ref[...]).astype(BF16)

    @pl.when(j < n_hy)
    def _():
        pT_ref[0] = lax.dot_general(
            whT_ref[0], a_ref[...], (((1,), (1,)), ((), ())),
            preferred_element_type=F32).astype(BF16)

    @pl.when(j >= n_hy)
    def _():
        pn_ref[...] = jnp.dot(a_ref[...], wn_ref[0],
                              preferred_element_type=F32).astype(BF16)


def _in_proj(x2, g, whT, wn, batch, seq, tm):
    ntok, d = x2.shape
    n_hy, tc, _ = whT.shape
    n_nat = wn.shape[0]
    tpb = seq // tm
    return pl.pallas_call(
        functools.partial(_in_proj_kernel, n_hy=n_hy),
        grid=(ntok // tm, n_hy + n_nat),
        in_specs=[
            pl.BlockSpec((tm, d), lambda i, j: (i, 0)),
            pl.BlockSpec((1, d), lambda i, j: (0, 0)),
            pl.BlockSpec((1, tc, d), lambda i, j: (jnp.minimum(j, n_hy - 1), 0, 0)),
            pl.BlockSpec((1, d, tc), lambda i, j: (jnp.maximum(j - n_hy, 0), 0, 0)),
        ],
        out_specs=[
            pl.BlockSpec((1, tc, tm),
                         lambda i, j: (i // tpb, jnp.minimum(j, n_hy - 1), i % tpb)),
            pl.BlockSpec((tm, tc), lambda i, j: (i, jnp.maximum(j - n_hy, 0))),
        ],
        out_shape=[
            jax.ShapeDtypeStruct((batch, n_hy * tc, seq), BF16),
            jax.ShapeDtypeStruct((ntok, n_nat * tc), BF16),
        ],
        scratch_shapes=[pltpu.VMEM((tm, d), BF16)],
        compiler_params=_cparams(("arbitrary", "arbitrary")),
        name="in_proj",
    )(x2, g, whT, wn)


def _fft_constants(r):
    n2f = 2 * r
    m = LANES * n2f
    k2 = np.arange(r, dtype=np.float64)[:, None]
    n2 = np.arange(r, dtype=np.float64)[None, :]
    th = 2.0 * np.pi * n2 * (k2 + 0.5) / n2f
    er, ei = np.cos(th), -np.sin(th)
    eye = np.eye(SUBLANES)
    bd_fwd = np.concatenate([np.kron(er, eye), np.kron(ei, eye)], axis=0)
    bd_inv = np.concatenate([np.kron(er.T, eye), np.kron(ei.T, eye)], axis=1) * (2.0 / m)
    n1 = np.arange(LANES, dtype=np.float64)[None, :]
    ph = 2.0 * np.pi * n1 * (k2 + 0.5) / m
    tw = np.stack([np.repeat(np.cos(ph), SUBLANES, axis=0),
                   np.repeat(-np.sin(ph), SUBLANES, axis=0)])
    a = np.arange(LANES, dtype=np.float64)
    ps = 2.0 * np.pi * np.outer(a, a) / LANES
    gr, gi = np.cos(ps), -np.sin(ps)
    g_fwd = np.block([[gr, gi], [-gi, gr]])
    g_inv = np.block([[gr, -gi], [gi, gr]])
    as_bf = lambda v: jnp.asarray(v, F32).astype(BF16)
    return as_bf(bd_fwd), as_bf(bd_inv), jnp.asarray(tw, F32), as_bf(g_fwd), as_bf(g_inv)


def _strip_to_pair(s):
    r = s.shape[1] // LANES
    return jnp.concatenate(
        [jnp.concatenate([s[:SUBLANES, n * LANES:(n + 1) * LANES],
                          s[SUBLANES:, n * LANES:(n + 1) * LANES]], axis=1)
         for n in range(r)], axis=0)


def _pair_to_strip(z):
    r = z.shape[0] // SUBLANES
    return jnp.concatenate(
        [jnp.concatenate([z[n * SUBLANES:(n + 1) * SUBLANES, :LANES],
                          z[n * SUBLANES:(n + 1) * SUBLANES, LANES:]], axis=0)
         for n in range(r)], axis=1)


def _fft_fwd(zs, bd_fwd, tr, ti, g_fwd):
    tr_rows = zs[0].shape[0]
    tr2 = jnp.concatenate([tr, tr], axis=1)
    ti2 = jnp.concatenate([ti, ti], axis=1)
    a_s = [jnp.dot(bd_fwd, z.astype(BF16), preferred_element_type=F32) for z in zs]
    b_s = []
    for a in a_s:
        ar, ai = a[:tr_rows], a[tr_rows:]
        br = ar * tr2 - ai * ti2
        bi = ar * ti2 + ai * tr2
        b_s.append(jnp.concatenate([
            jnp.concatenate([br[:, :LANES], bi[:, :LANES]], axis=1),
            jnp.concatenate([br[:, LANES:], bi[:, LANES:]], axis=1)], axis=0).astype(BF16))
    return [jnp.dot(b, g_fwd, preferred_element_type=F32) for b in b_s]


def _fft_inv(ys, bd_inv, tr, ti, g_inv):
    tr_rows = ys[0].shape[0] // 2
    trs = jnp.concatenate([tr, tr], axis=0)
    tis = jnp.concatenate([ti, ti], axis=0)
    p_s = [jnp.dot(y.astype(BF16), g_inv, preferred_element_type=F32) for y in ys]
    q_s = []
    for p in p_s:
        pr, pi = p[:, :LANES], p[:, LANES:]
        qr = pr * trs + pi * tis
        qi = pi * trs - pr * tis
        q_s.append(jnp.concatenate([
            jnp.concatenate([qr[:tr_rows], qr[tr_rows:]], axis=1),
            jnp.concatenate([qi[:tr_rows], qi[tr_rows:]], axis=1)], axis=0).astype(BF16))
    return [jnp.dot(bd_inv, q, preferred_element_type=F32) for q in q_s]


def _filter_mlp_kernel(zT_ref, t_ref, w1_ref, b1_ref, w2_ref, b2_ref, w3_ref, b3_ref,
                       fr_ref, w4_ref, ad_ref, o_ref, *, chunk):
    hp = lax.Precision.HIGHEST
    tl = zT_ref.shape[1]
    fr = fr_ref[...]
    h = jnp.sin(fr * (jnp.dot(w1_ref[...], zT_ref[...], precision=hp,
                              preferred_element_type=F32) + b1_ref[...]))
    h = jnp.sin(fr * (jnp.dot(w2_ref[...], h, precision=hp,
                              preferred_element_type=F32) + b2_ref[...]))
    h = jnp.sin(fr * (jnp.dot(w3_ref[...], h, precision=hp,
                              preferred_element_type=F32) + b3_ref[...]))
    t = t_ref[...]
    c = ad_ref.shape[0]
    first = (pl.program_id(0) == 0) & (lax.broadcasted_iota(jnp.int32, (chunk, tl), 1) == 0)
    for cc in range(c // chunk):
        ad = pltpu.repeat(ad_ref[cc * chunk:(cc + 1) * chunk, :], tl // LANES, axis=1)
        decay = jnp.exp(-(t * ad))
        for od in range(2 * HY_ORDER):
            r0 = od * c + cc * chunk
            v = jnp.dot(w4_ref[r0:r0 + chunk, :], h, precision=hp,
                        preferred_element_type=F32) * decay
            if od % 2 == 1:
                v = jnp.where(first, 0.0, v)
            o_ref[r0:r0 + chunk, :] = v


def _filter_mlp(zT, tvec, w1T, b1, w2T, b2, w3T, b3, fr, w4T, ad, tl):
    nz, seq = zT.shape
    rows, hid = w4T.shape
    full = lambda a: pl.BlockSpec(a.shape, lambda i: (0,) * a.ndim)
    return pl.pallas_call(
        functools.partial(_filter_mlp_kernel, chunk=512),
        grid=(seq // tl,),
        in_specs=[
            pl.BlockSpec((nz, tl), lambda i: (0, i)),
            pl.BlockSpec((1, tl), lambda i: (0, i)),
            full(w1T), full(b1), full(w2T), full(b2), full(w3T), full(b3), full(fr),
            full(w4T), full(ad),
        ],
        out_specs=pl.BlockSpec((rows, tl), lambda i: (0, i)),
        out_shape=jax.ShapeDtypeStruct((rows, seq), F32),
        compiler_params=_cparams(("arbitrary",)),
        name="filter_mlp",
    )(zT, tvec, w1T, b1, w2T, b2, w3T, b3, fr, w4T, ad)


def _filter_fft_kernel(hf_ref, hb_ref, bdf_ref, tw_ref, gf_ref, o_ref):
    nstrips = hf_ref.shape[0] // STRIP

    def body(s, carry):
        c0 = pl.multiple_of(s * STRIP, STRIP)
        xf, xb = _fft_fwd(
            [_strip_to_pair(ref[pl.ds(c0, STRIP), :]) for ref in (hf_ref, hb_ref)],
            bdf_ref[...], tw_ref[0], tw_ref[1], gf_ref[...])
        o_ref[0, s] = jnp.concatenate(
            [xf[:, :LANES] + xb[:, :LANES], xf[:, LANES:] - xb[:, LANES:]], axis=1)
        return carry

    lax.fori_loop(0, nstrips, body, 0)


def _filter_fft(hT, consts, c, r, cs):
    bd_fwd, _, tw, g_fwd, _ = consts
    seq = hT.shape[1]
    nblk = c // cs
    full = lambda a: pl.BlockSpec(a.shape, lambda o, j: (0,) * a.ndim)
    return pl.pallas_call(
        _filter_fft_kernel,
        grid=(HY_ORDER, nblk),
        in_specs=[
            pl.BlockSpec((cs, seq), lambda o, j: ((2 * o) * nblk + j, 0)),
            pl.BlockSpec((cs, seq), lambda o, j: ((2 * o + 1) * nblk + j, 0)),
            full(bd_fwd), full(tw), full(g_fwd),
        ],
        out_specs=pl.BlockSpec((1, cs // STRIP, STRIP * r, 2 * LANES),
                               lambda o, j: (o, j, 0, 0)),
        out_shape=jax.ShapeDtypeStruct((HY_ORDER, c // STRIP, STRIP * r, 2 * LANES), F32),
        compiler_params=_cparams(("arbitrary", "arbitrary")),
        name="filter_fft",
    )(hT, hT, bd_fwd, tw, g_fwd)


def _hyena_kernel(v_ref, x1_ref, x2_ref, par_ref, kh_ref, bdf_ref, bdi_ref, tw_ref, gf_ref,
                  gi_ref, o_ref, *, strips_per_iter):
    seq = v_ref.shape[2]
    r = seq // LANES
    nstrips = v_ref.shape[1] // STRIP
    t_idx = lax.broadcasted_iota(jnp.int32, (STRIP, seq), 1)
    seq_first = t_idx == 0
    seq_last = t_idx == seq - 1

    def short_conv(p, w):
        prev = jnp.where(seq_first, 0.0, pltpu.roll(p, 1, axis=1))
        nxt = jnp.where(seq_last, 0.0, pltpu.roll(p, seq - 1, axis=1))
        wide = lambda k: pltpu.repeat(w[k], r, axis=1)
        return prev * wide(0) + p * wide(1) + nxt * wide(2) + wide(3)

    def body(it, carry):
        strips = [it * strips_per_iter + k for k in range(strips_per_iter)]
        rows = [pl.ds(pl.multiple_of(s * STRIP, STRIP), STRIP) for s in strips]
        tr, ti = tw_ref[0], tw_ref[1]
        part = lambda ref, rw, k0: _strip_to_pair(short_conv(
            ref[0, rw, :].astype(F32), [par_ref[k0 + k, rw, :] for k in range(4)]))
        zs = [part(v_ref, rw, 0) for rw in rows]
        gates = (x1_ref, x2_ref)
        for o in range(HY_ORDER):
            gs = [part(gates[o], rw, 4 + 4 * o) for rw in rows]
            xs = _fft_fwd(zs, bdf_ref[...], tr, ti, gf_ref[...])
            ys = []
            for s, x in zip(strips, xs):
                xr, xi = x[:, :LANES], x[:, LANES:]
                kr = kh_ref[o, s, :, :LANES]
                ki = kh_ref[o, s, :, LANES:]
                ys.append(jnp.concatenate([xr * kr - xi * ki, xr * ki + xi * kr], axis=1))
            convs = _fft_inv(ys, bdi_ref[...], tr, ti, gi_ref[...])
            nxt = []
            for rw, z, gate, conv in zip(rows, zs, gs, convs):
                b = par_ref[12 + o, rw, :]
                bias = pltpu.repeat(
                    jnp.concatenate([b[:SUBLANES], b[SUBLANES:]], axis=1), r, axis=0)
                nxt.append(gate * (conv + z * bias))
            zs = nxt
        for rw, z in zip(rows, zs):
            o_ref[0, rw, :] = _pair_to_strip(z).astype(o_ref.dtype)
        return carry

    lax.fori_loop(0, nstrips // strips_per_iter, body, 0)


def _hyena(pT, par, khat, consts, batch, c, cs, strips_per_iter):
    bd_fwd, bd_inv, tw, g_fwd, g_inv = consts
    seq = pT.shape[2]
    nblk = c // cs
    full = lambda a: pl.BlockSpec(a.shape, lambda j, b: (0,) * a.ndim)
    part = lambda k: pl.BlockSpec((1, cs, seq), lambda j, b, k=k: (b, k * nblk + j, 0))
    return pl.pallas_call(
        functools.partial(_hyena_kernel, strips_per_iter=strips_per_iter),
        grid=(nblk, batch),
        in_specs=[
            part(0), part(1), part(2),
            pl.BlockSpec((par.shape[0], cs, LANES), lambda j, b: (0, j, 0)),
            pl.BlockSpec((HY_ORDER, cs // STRIP) + khat.shape[2:], lambda j, b: (0, j, 0, 0)),
            full(bd_fwd), full(bd_inv), full(tw), full(g_fwd), full(g_inv),
        ],
        out_specs=pl.BlockSpec((1, cs, seq), lambda j, b: (b, j, 0)),
        out_shape=jax.ShapeDtypeStruct((batch, c, seq), BF16),
        compiler_params=_cparams(("arbitrary", "arbitrary")),
        name="hyena",
    )(pT, pT, pT, par, khat, bd_fwd, bd_inv, tw, g_fwd, g_inv)


def _nat_kernel(q_ref, k_ref, v_ref, bias_ref, o_ref, *, rows, scale, rows_per_iter):
    npair = q_ref.shape[2] // LANES
    win = NAT_KH * GRID_W
    lo = lax.broadcasted_iota(jnp.int32, (GRID_W, LANES), 1) < LANES // 2

    def body(it, carry):
        chains = []
        for u in range(rows_per_iter):
            i = it * rows_per_iter + u
            rs = jnp.clip(i - NAT_KH // 2, 0, rows - NAT_KH)
            d = rs - i + (NAT_KH - 1)
            q0 = pl.multiple_of(i * GRID_W, GRID_W)
            k0 = pl.multiple_of(rs * GRID_W, GRID_W)
            for p in range(npair):
                chains.append((d, q0, k0, p, slice(p * LANES, (p + 1) * LANES)))
        scores = []
        for d, q0, k0, p, cols in chains:
            qp = q_ref[0, pl.ds(q0, GRID_W), cols].astype(F32) * scale
            q2 = jnp.concatenate([jnp.where(lo, qp, 0.0), jnp.where(lo, 0.0, qp)], axis=0)
            s = lax.dot_general(q2.astype(BF16), k_ref[0, pl.ds(k0, win), cols],
                                (((1,), (1,)), ((), ())), preferred_element_type=F32)
            scores.append(s + bias_ref[d, p])
        probs = []
        for s in scores:
            m = jnp.max(s, axis=-1, keepdims=True)
            pe = jnp.exp(s - m)
            probs.append((pe.astype(BF16), jnp.sum(pe, axis=-1, keepdims=True)))
        for (d, q0, k0, p, cols), (pe, l) in zip(chains, probs):
            o = jnp.dot(pe, v_ref[0, pl.ds(k0, win), cols], preferred_element_type=F32) / l
            o_ref[0, pl.ds(q0, GRID_W), cols] = jnp.where(
                lo, o[:GRID_W], o[GRID_W:]).astype(o_ref.dtype)
        return carry

    lax.fori_loop(0, rows // rows_per_iter, body, 0)


def _nat(pn3, bias, batch, seq, width, hd):
    gw = 2 * LANES
    ng = width // gw
    rows = seq // GRID_W
    blk = lambda k: pl.BlockSpec((1, seq, gw), lambda b, hg, k=k: (b, 0, k * ng + hg))
    return pl.pallas_call(
        functools.partial(_nat_kernel, rows=rows, scale=hd ** -0.5, rows_per_iter=4),
        grid=(batch, ng),
        in_specs=[
            blk(0), blk(1), blk(2),
            pl.BlockSpec((NAT_KH, gw // LANES, 2 * GRID_W, NAT_KH * GRID_W),
                         lambda b, hg: (0, hg, 0, 0)),
        ],
        out_specs=pl.BlockSpec((1, seq, gw), lambda b, hg: (b, 0, hg)),
        out_shape=jax.ShapeDtypeStruct((batch, seq, width), BF16),
        compiler_params=_cparams(("arbitrary", "arbitrary")),
        name="nat",
    )(pn3, pn3, pn3, bias)


def _nat_bias_table(rpb):
    h = rpb.shape[0]
    cols = np.arange(GRID_W)
    col_start = np.clip(cols - NAT_KW // 2, 0, GRID_W - NAT_KW)
    cc = cols[None, :]
    valid = (cc >= col_start[:, None]) & (cc < col_start[:, None] + NAT_KW)
    col_off = np.clip(cc - cols[:, None] + (NAT_KW - 1), 0, 2 * NAT_KW - 2)
    row_off = np.arange(NAT_KH)[:, None] + np.arange(NAT_KH)[None, :]
    t = rpb.astype(F32)[:, row_off]
    t = t[:, :, :, col_off]
    t = jnp.where(valid[None, None, None], t, MASK_VALUE)
    t = jnp.transpose(t, (1, 0, 3, 2, 4))
    return t.reshape(NAT_KH, h // 2, 2 * GRID_W, NAT_KH * GRID_W)


def _out_proj_kernel(x_ref, yh_ref, yn_ref, gh_ref, gn_ref, woh_ref, won_ref, h_ref):
    tm = x_ref.shape[0]
    yh = yh_ref[0].astype(F32)
    ms = jnp.mean(yh * yh, axis=0, keepdims=True)
    gh = pltpu.repeat(gh_ref[...], tm // LANES, axis=1)
    nh = (yh * lax.rsqrt(ms + NORM_EPS) * gh).astype(BF16)
    nn = _rms(yn_ref[...].astype(F32), gn_ref[...]).astype(BF16)
    acc = lax.dot_general(nh, woh_ref[...], (((0,), (0,)), ((), ())),
                          preferred_element_type=F32)
    acc = acc + jnp.dot(nn, won_ref[...], preferred_element_type=F32)
    h_ref[...] = x_ref[...] + acc


def _out_proj(x2, yhT, yn, gh, gn, woh, won, seq, tm):
    ntok, d = x2.shape
    c = yn.shape[1]
    tpb = seq // tm
    full = lambda a: pl.BlockSpec(a.shape, lambda i: (0,) * a.ndim)
    return pl.pallas_call(
        _out_proj_kernel,
        grid=(ntok // tm,),
        in_specs=[
            pl.BlockSpec((tm, d), lambda i: (i, 0)),
            pl.BlockSpec((1, c, tm), lambda i: (i // tpb, 0, i % tpb)),
            pl.BlockSpec((tm, c), lambda i: (i, 0)),
            full(gh), full(gn), full(woh), full(won),
        ],
        out_specs=pl.BlockSpec((tm, d), lambda i: (i, 0)),
        out_shape=jax.ShapeDtypeStruct((ntok, d), F32),
        compiler_params=_cparams(("arbitrary",)),
        name="out_proj",
    )(x2, yhT, yn, gh, gn, woh, won)


def _mlp_kernel(h_ref, g_ref, wu_ref, wd_ref, gf_ref, o_ref, a_ref):
    j = pl.program_id(1)

    @pl.when(j == 0)
    def _():
        h = h_ref[...]
        a_ref[...] = _rms(h, g_ref[...]).astype(BF16)
        o_ref[...] = h

    u = jnp.dot(a_ref[...], wu_ref[...], preferred_element_type=F32)
    u = jnp.square(jnp.maximum(u, 0.0)).astype(BF16)
    o_ref[...] += jnp.dot(u, wd_ref[...], preferred_element_type=F32)

    @pl.when(j == pl.num_programs(1) - 1)
    def _():
        o_ref[...] = _rms(o_ref[...], gf_ref[...])


def _mlp(h, g, wu, wd, gf, tm, tf):
    ntok, d = h.shape
    dff = wu.shape[1]
    return pl.pallas_call(
        _mlp_kernel,
        grid=(ntok // tm, dff // tf),
        in_specs=[
            pl.BlockSpec((tm, d), lambda i, j: (i, 0)),
            pl.BlockSpec((1, d), lambda i, j: (0, 0)),
            pl.BlockSpec((d, tf), lambda i, j: (0, j)),
            pl.BlockSpec((tf, d), lambda i, j: (j, 0)),
            pl.BlockSpec((1, d), lambda i, j: (0, 0)),
        ],
        out_specs=pl.BlockSpec((tm, d), lambda i, j: (i, 0)),
        out_shape=jax.ShapeDtypeStruct((ntok, d), F32),
        scratch_shapes=[pltpu.VMEM((tm, d), BF16)],
        compiler_params=_cparams(("arbitrary", "arbitrary")),
        name="mlp",
    )(h, g, wu, wd, gf)


def _positional_features(seq, emb):
    bands = (emb - 1) // 2
    t = np.linspace(0.0, 1.0, seq)
    w_ang = (2.0 * np.pi / seq) * np.arange(seq)
    f = np.linspace(1e-4, bands - 1, bands)
    ang = w_ang[None, :] * f[:, None]
    z = np.concatenate([t[None, :], np.cos(ang), -np.sin(ang)], axis=0)
    pad = (-z.shape[0]) % SUBLANES
    z = np.concatenate([z, np.zeros((pad, seq))], axis=0)
    return jnp.asarray(z, F32), jnp.asarray(t[None, :], F32)


def _lane_rep(v):
    return jnp.broadcast_to(v.astype(F32)[..., None], v.shape + (LANES,))


def _trunk(x, w):
    batch, seq, d = x.shape
    c = w["hy_bias"].shape[-1]
    r = seq // LANES
    rows = seq // GRID_W
    assert seq % LANES == 0 and c % STRIP == 0
    assert rows >= NAT_KH and seq % GRID_W == 0
    consts = _fft_constants(r)
    cs = 64

    emb = w["hy_pe_w1"].shape[0]
    zT, tvec = _positional_features(seq, emb)
    w1T = jnp.pad(w["hy_pe_w1"].astype(F32).T, ((0, 0), (0, zT.shape[0] - emb)))
    col = lambda v: v.astype(F32)[:, None]
    max_decay = math.log(HY_DECAY_TARGET) / HY_FAST_DECAY
    min_decay = math.log(HY_DECAY_TARGET) / HY_SLOW_DECAY
    absd = jnp.asarray(np.abs(np.linspace(min_decay, max_decay, c)), F32)
    hT = _filter_mlp(zT, tvec, w1T, col(w["hy_pe_b1"]), w["hy_pe_w2"].astype(F32).T,
                     col(w["hy_pe_b2"]), w["hy_pe_w3"].astype(F32).T, col(w["hy_pe_b3"]),
                     col(w["hy_pe_freq"]), w["hy_pe_w4"].astype(F32).T, _lane_rep(absd),
                     tl=512)
    khat = _filter_fft(hT, consts, c, r, cs)

    x2 = x.reshape(batch * seq, d)
    w_in = w["w_in"]
    n_hy = 3 * c
    tc = 1024
    whT = w_in[:, :n_hy].T.astype(BF16).reshape(n_hy // tc, tc, d)
    wn = w_in[:, n_hy:].astype(BF16).reshape(d, -1, tc).transpose(1, 0, 2)
    pT, pn = _in_proj(x2, w["norm_mix_g"].astype(F32)[None, :], whT, wn, batch, seq, tm=1024)

    cw = w["hy_conv_w"].astype(F32).reshape(3, 3, c)
    cb = w["hy_conv_b"].astype(F32).reshape(3, c)
    par = [jnp.concatenate([cw[:, p], cb[p][None]], axis=0) for p in range(3)]
    par = _lane_rep(jnp.concatenate(par + [w["hy_bias"].astype(F32)], axis=0))
    yhT = _hyena(pT, par, khat, consts, batch, c, cs, strips_per_iter=4 if r <= 32 else 2)

    nat_w = pn.shape[1] // 3
    bias = _nat_bias_table(w["nat_rpb"])
    yn = _nat(pn.reshape(batch, seq, 3 * nat_w), bias, batch, seq, nat_w, nat_w // NAT_HEADS)
    yn = yn.reshape(batch * seq, nat_w)

    w_out = w["w_out"].astype(BF16)
    h = _out_proj(x2, yhT, yn, _lane_rep(w["gnorm_hy"]), w["gnorm_nat"].astype(F32)[None, :],
                  w_out[:c], w_out[c:], seq, tm=512)
    y = _mlp(h, w["norm_mlp_g"].astype(F32)[None, :], w["w_up"].astype(BF16),
             w["w_down"].astype(BF16), w["norm_f_g"].astype(F32)[None, :], tm=512, tf=1024)
    return y.reshape(batch, seq, d)


def kernel(x_prompt, x_sample, norm_mix_g, w_in, hy_conv_w, hy_conv_b, hy_pe_w1, hy_pe_b1,
           hy_pe_w2, hy_pe_b2, hy_pe_w3, hy_pe_b3, hy_pe_freq, hy_pe_w4, hy_bias, nat_rpb,
           gnorm_hy, gnorm_nat, w_out, norm_mlp_g, w_up, w_down, norm_f_g):
    assert norm_mix_g.shape[0] == 1, "single-layer trunk"
    w = dict(norm_mix_g=norm_mix_g[0], w_in=w_in[0], hy_conv_w=hy_conv_w[0],
             hy_conv_b=hy_conv_b[0], hy_pe_w1=hy_pe_w1[0], hy_pe_b1=hy_pe_b1[0],
             hy_pe_w2=hy_pe_w2[0], hy_pe_b2=hy_pe_b2[0], hy_pe_w3=hy_pe_w3[0],
             hy_pe_b3=hy_pe_b3[0], hy_pe_freq=hy_pe_freq[0], hy_pe_w4=hy_pe_w4[0],
             hy_bias=hy_bias[0], nat_rpb=nat_rpb[0], gnorm_hy=gnorm_hy[0],
             gnorm_nat=gnorm_nat[0], w_out=w_out[0], norm_mlp_g=norm_mlp_g[0],
             w_up=w_up[0], w_down=w_down[0], norm_f_g=norm_f_g)
    return (_trunk(x_prompt, w), _trunk(x_sample, w))
```

```python
import functools
import math

import numpy as np
import jax
import jax.numpy as jnp
from jax import lax
from jax.experimental import pallas as pl
from jax.experimental.pallas import tpu as pltpu

F32 = jnp.float32
BF16 = jnp.bfloat16

LANES = 128
SUBLANES = 8
STRIP = 2 * SUBLANES
VMEM_LIMIT = 56 * 1024 * 1024

NORM_EPS = 1e-5
GRID_W = 64
NAT_HEADS = 16
NAT_KH = 8
NAT_KW = 16
HY_ORDER = 2
HY_FAST_DECAY = 0.3
HY_SLOW_DECAY = 1.5
HY_DECAY_TARGET = 1e-2
MASK_VALUE = -1e30


def _cparams(sem):
    return pltpu.CompilerParams(dimension_semantics=sem, vmem_limit_bytes=VMEM_LIMIT)


def _rms(x, g):
    ms = jnp.mean(x * x, axis=-1, keepdims=True)
    return x * lax.rsqrt(ms + NORM_EPS) * g


def _in_proj_kernel(xa_ref, xb_ref, g_ref, whT_ref, wn_ref, pT_ref, pn_ref, a_ref, *, n_hy):
    j = pl.program_id(1)

    @pl.when(j == 0)
    def _():
        half = xa_ref.shape[1]
        xa, xb = xa_ref[...], xb_ref[...]
        ss = jnp.sum(xa * xa, axis=-1, keepdims=True) + jnp.sum(xb * xb, axis=-1, keepdims=True)
        inv = lax.rsqrt(ss / (2 * half) + NORM_EPS)
        a_ref[:, :half] = (xa * inv * g_ref[:, :half]).astype(BF16)
        a_ref[:, half:] = (xb * inv * g_ref[:, half:]).astype(BF16)

    @pl.when(j < n_hy)
    def _():
        pT_ref[0] = lax.dot_general(
            whT_ref[0], a_ref[...], (((1,), (1,)), ((), ())),
            preferred_element_type=F32).astype(BF16)

    @pl.when(j >= n_hy)
    def _():
        pn_ref[...] = jnp.dot(a_ref[...], wn_ref[0],
                              preferred_element_type=F32).astype(BF16)


def _in_proj(x2, g, whT, wn, batch, seq, tm):
    ntok, d = x2.shape
    n_hy, tc, _ = whT.shape
    n_nat = wn.shape[0]
    tpb = seq // tm
    nt = ntok // tm
    ahead = lambda i, j, at: jnp.minimum(i + (j >= at).astype(jnp.int32), nt - 1)
    return pl.pallas_call(
        functools.partial(_in_proj_kernel, n_hy=n_hy),
        grid=(nt, n_hy + n_nat),
        in_specs=[
            pl.BlockSpec((tm, d // 2), lambda i, j: (ahead(i, j, 1), 0)),
            pl.BlockSpec((tm, d // 2), lambda i, j: (ahead(i, j, 2), 1)),
            pl.BlockSpec((1, d), lambda i, j: (0, 0)),
            pl.BlockSpec((1, tc, d), lambda i, j: (jnp.minimum(j, n_hy - 1), 0, 0)),
            pl.BlockSpec((1, d, tc), lambda i, j: (jnp.maximum(j - n_hy, 0), 0, 0)),
        ],
        out_specs=[
            pl.BlockSpec((1, tc, tm),
                         lambda i, j: (i // tpb, jnp.minimum(j, n_hy - 1), i % tpb)),
            pl.BlockSpec((tm, tc), lambda i, j: (i, jnp.maximum(j - n_hy, 0))),
        ],
        out_shape=[
            jax.ShapeDtypeStruct((batch, n_hy * tc, seq), BF16),
            jax.ShapeDtypeStruct((ntok, n_nat * tc), BF16),
        ],
        scratch_shapes=[pltpu.VMEM((tm, d), BF16)],
        compiler_params=_cparams(("arbitrary", "arbitrary")),
        name="in_proj",
    )(x2, x2, g, whT, wn)


def _fft_constants(r):
    n2f = 2 * r
    m = LANES * n2f
    k2 = np.arange(r, dtype=np.float64)[:, None]
    n2 = np.arange(r, dtype=np.float64)[None, :]
    th = 2.0 * np.pi * n2 * (k2 + 0.5) / n2f
    er, ei = np.cos(th), -np.sin(th)
    eye = np.eye(SUBLANES)
    bd_fwd = np.concatenate([np.kron(er, eye), np.kron(ei, eye)], axis=0)
    bd_inv = np.concatenate([np.kron(er.T, eye), np.kron(ei.T, eye)], axis=1) * (2.0 / m)
    n1 = np.arange(LANES, dtype=np.float64)[None, :]
    ph = 2.0 * np.pi * n1 * (k2 + 0.5) / m
    tw = np.stack([np.repeat(np.cos(ph), SUBLANES, axis=0),
                   np.repeat(-np.sin(ph), SUBLANES, axis=0)])
    a = np.arange(LANES, dtype=np.float64)
    ps = 2.0 * np.pi * np.outer(a, a) / LANES
    gr, gi = np.cos(ps), -np.sin(ps)
    g_fwd = np.block([[gr, gi], [-gi, gr]])
    g_inv = np.block([[gr, -gi], [gi, gr]])
    as_bf = lambda v: jnp.asarray(v, F32).astype(BF16)
    return as_bf(bd_fwd), as_bf(bd_inv), as_bf(tw), as_bf(g_fwd), as_bf(g_inv)


def _strip_to_pair(s):
    r = s.shape[1] // LANES
    return jnp.concatenate(
        [jnp.concatenate([s[:SUBLANES, n * LANES:(n + 1) * LANES],
                          s[SUBLANES:, n * LANES:(n + 1) * LANES]], axis=1)
         for n in range(r)], axis=0)


def _pair_to_strip(z):
    r = z.shape[0] // SUBLANES
    return jnp.concatenate(
        [jnp.concatenate([z[n * SUBLANES:(n + 1) * SUBLANES, :LANES],
                          z[n * SUBLANES:(n + 1) * SUBLANES, LANES:]], axis=0)
         for n in range(r)], axis=1)


def _fft_fwd(zs, bd_fwd, tr, ti, g_fwd):
    tr_rows = zs[0].shape[0]
    tr2 = jnp.concatenate([tr, tr], axis=1)
    ti2 = jnp.concatenate([ti, ti], axis=1)
    a_s = [jnp.dot(bd_fwd, z.astype(BF16), preferred_element_type=F32) for z in zs]
    b_s = []
    for a in a_s:
        a = a.astype(BF16)
        ar, ai = a[:tr_rows], a[tr_rows:]
        br = ar * tr2 - ai * ti2
        bi = ar * ti2 + ai * tr2
        b_s.append(jnp.concatenate([
            jnp.concatenate([br[:, :LANES], bi[:, :LANES]], axis=1),
            jnp.concatenate([br[:, LANES:], bi[:, LANES:]], axis=1)], axis=0))
    return [jnp.dot(b, g_fwd, preferred_element_type=F32) for b in b_s]


def _cmul_spectrum(x, kr, ki):
    x = x.astype(BF16)
    xr, xi = x[:, :LANES], x[:, LANES:]
    return jnp.concatenate([xr * kr - xi * ki, xr * ki + xi * kr], axis=1)


def _fft_inv(ys, bd_inv, tr, ti, g_inv):
    tr_rows = ys[0].shape[0] // 2
    trs = jnp.concatenate([tr, tr], axis=0)
    tis = jnp.concatenate([ti, ti], axis=0)
    p_s = [jnp.dot(y, g_inv, preferred_element_type=F32) for y in ys]
    q_s = []
    for p in p_s:
        p = p.astype(BF16)
        pr, pi = p[:, :LANES], p[:, LANES:]
        qr = pr * trs + pi * tis
        qi = pi * trs - pr * tis
        q_s.append(jnp.concatenate([
            jnp.concatenate([qr[:tr_rows], qr[tr_rows:]], axis=1),
            jnp.concatenate([qi[:tr_rows], qi[tr_rows:]], axis=1)], axis=0))
    return [jnp.dot(bd_inv, q, preferred_element_type=F32) for q in q_s]


def _filter_mlp_kernel(zT_ref, t_ref, w1_ref, b1_ref, w2_ref, b2_ref, w3_ref, b3_ref,
                       fr_ref, w4_ref, ad_ref, o_ref, *, chunk):
    hp = lax.Precision.HIGHEST
    tl = zT_ref.shape[1]
    fr = fr_ref[...]
    h = jnp.sin(fr * (jnp.dot(w1_ref[...], zT_ref[...], precision=hp,
                              preferred_element_type=F32) + b1_ref[...]))
    h = jnp.sin(fr * (jnp.dot(w2_ref[...], h, precision=hp,
                              preferred_element_type=F32) + b2_ref[...]))
    h = jnp.sin(fr * (jnp.dot(w3_ref[...], h, precision=hp,
                              preferred_element_type=F32) + b3_ref[...]))
    t = t_ref[...]
    c = ad_ref.shape[0]
    first = (pl.program_id(0) == 0) & (lax.broadcasted_iota(jnp.int32, (chunk, tl), 1) == 0)
    for cc in range(c // chunk):
        ad = jnp.concatenate([ad_ref[cc * chunk:(cc + 1) * chunk, :]] * (tl // LANES), axis=1)
        decay = jnp.exp(-(t * ad))
        for od in range(2 * HY_ORDER):
            r0 = od * c + cc * chunk
            v = jnp.dot(w4_ref[r0:r0 + chunk, :], h, precision=hp,
                        preferred_element_type=F32) * decay
            if od % 2 == 1:
                v = jnp.where(first, 0.0, v)
            o_ref[r0:r0 + chunk, :] = v


def _filter_mlp(zT, tvec, w1T, b1, w2T, b2, w3T, b3, fr, w4T, ad, tl):
    nz, seq = zT.shape
    rows, hid = w4T.shape
    full = lambda a: pl.BlockSpec(a.shape, lambda i: (0,) * a.ndim)
    return pl.pallas_call(
        functools.partial(_filter_mlp_kernel, chunk=512),
        grid=(seq // tl,),
        in_specs=[
            pl.BlockSpec((nz, tl), lambda i: (0, i)),
            pl.BlockSpec((1, tl), lambda i: (0, i)),
            full(w1T), full(b1), full(w2T), full(b2), full(w3T), full(b3), full(fr),
            full(w4T), full(ad),
        ],
        out_specs=pl.BlockSpec((rows, tl), lambda i: (0, i)),
        out_shape=jax.ShapeDtypeStruct((rows, seq), F32),
        compiler_params=_cparams(("arbitrary",)),
        name="filter_mlp",
    )(zT, tvec, w1T, b1, w2T, b2, w3T, b3, fr, w4T, ad)


def _filter_fft_kernel(hf_ref, hb_ref, bdf_ref, tw_ref, gf_ref, o_ref):
    nstrips = hf_ref.shape[0] // STRIP

    def body(s, carry):
        c0 = pl.multiple_of(s * STRIP, STRIP)
        xf, xb = _fft_fwd(
            [_strip_to_pair(ref[pl.ds(c0, STRIP), :]) for ref in (hf_ref, hb_ref)],
            bdf_ref[...], tw_ref[0], tw_ref[1], gf_ref[...])
        o_ref[0, s] = jnp.concatenate(
            [xf[:, :LANES] + xb[:, :LANES], xf[:, LANES:] - xb[:, LANES:]],
            axis=1).astype(o_ref.dtype)
        return carry

    lax.fori_loop(0, nstrips, body, 0)


def _filter_fft(hT, consts, c, r, cs):
    bd_fwd, _, tw, g_fwd, _ = consts
    seq = hT.shape[1]
    nblk = c // cs
    full = lambda a: pl.BlockSpec(a.shape, lambda o, j: (0,) * a.ndim)
    return pl.pallas_call(
        _filter_fft_kernel,
        grid=(HY_ORDER, nblk),
        in_specs=[
            pl.BlockSpec((cs, seq), lambda o, j: ((2 * o) * nblk + j, 0)),
            pl.BlockSpec((cs, seq), lambda o, j: ((2 * o + 1) * nblk + j, 0)),
            full(bd_fwd), full(tw), full(g_fwd),
        ],
        out_specs=pl.BlockSpec((1, cs // STRIP, STRIP * r, 2 * LANES),
                               lambda o, j: (o, j, 0, 0)),
        out_shape=jax.ShapeDtypeStruct((HY_ORDER, c // STRIP, STRIP * r, 2 * LANES), BF16),
        compiler_params=_cparams(("arbitrary", "arbitrary")),
        name="filter_fft",
    )(hT, hT, bd_fwd, tw, g_fwd)


def _hyena_kernel(v_ref, x1_ref, x2_ref, par_ref, kh_ref, bdf_ref, bdi_ref, tw_ref, gf_ref,
                  gi_ref, o_ref, *, strips_per_iter):
    seq = v_ref.shape[2]
    r = seq // LANES
    nstrips = v_ref.shape[1] // STRIP
    lane = lax.broadcasted_iota(jnp.int32, (STRIP, LANES), 1)

    def short_conv(p, w):
        prev = pltpu.roll(p, 1, axis=1)
        nxt = pltpu.roll(p, seq - 1, axis=1)
        prev = jnp.concatenate(
            [jnp.where(lane == 0, 0.0, prev[:, :LANES]), prev[:, LANES:]], axis=1)
        nxt = jnp.concatenate(
            [nxt[:, :seq - LANES], jnp.where(lane == LANES - 1, 0.0, nxt[:, seq - LANES:])],
            axis=1)
        wide = lambda k: jnp.concatenate([w[k]] * r, axis=1)
        return prev * wide(0) + p * wide(1) + nxt * wide(2) + wide(3)

    def body(it, carry):
        strips = [it * strips_per_iter + k for k in range(strips_per_iter)]
        rows = [pl.ds(pl.multiple_of(s * STRIP, STRIP), STRIP) for s in strips]
        tr, ti = tw_ref[0], tw_ref[1]
        part = lambda ref, rw, k0: _strip_to_pair(short_conv(
            ref[0, rw, :].astype(F32), [par_ref[k0 + k, rw, :] for k in range(4)]))
        zs = [part(v_ref, rw, 0) for rw in rows]
        gates = (x1_ref, x2_ref)
        for o in range(HY_ORDER):
            gs = [part(gates[o], rw, 4 + 4 * o) for rw in rows]
            xs = _fft_fwd(zs, bdf_ref[...], tr, ti, gf_ref[...])
            ys = [_cmul_spectrum(x, kh_ref[o, s, :, :LANES], kh_ref[o, s, :, LANES:])
                  for s, x in zip(strips, xs)]
            convs = _fft_inv(ys, bdi_ref[...], tr, ti, gi_ref[...])
            nxt = []
            for rw, z, gate, conv in zip(rows, zs, gs, convs):
                b = par_ref[12 + o, rw, :]
                bias = jnp.concatenate(
                    [jnp.concatenate([b[:SUBLANES], b[SUBLANES:]], axis=1)] * r, axis=0)
                nxt.append(gate * (conv + z * bias))
            zs = nxt
        for rw, z in zip(rows, zs):
            o_ref[0, rw, :] = _pair_to_strip(z).astype(o_ref.dtype)
        return carry

    lax.fori_loop(0, nstrips // strips_per_iter, body, 0)


def _hyena(pT, par, khat, consts, batch, c, cs, strips_per_iter):
    bd_fwd, bd_inv, tw, g_fwd, g_inv = consts
    seq = pT.shape[2]
    nblk = c // cs
    full = lambda a: pl.BlockSpec(a.shape, lambda j, b: (0,) * a.ndim)
    part = lambda k: pl.BlockSpec((1, cs, seq), lambda j, b, k=k: (b, k * nblk + j, 0))
    return pl.pallas_call(
        functools.partial(_hyena_kernel, strips_per_iter=strips_per_iter),
        grid=(nblk, batch),
        in_specs=[
            part(0), part(1), part(2),
            pl.BlockSpec((par.shape[0], cs, LANES), lambda j, b: (0, j, 0)),
            pl.BlockSpec((HY_ORDER, cs // STRIP) + khat.shape[2:], lambda j, b: (0, j, 0, 0)),
            full(bd_fwd), full(bd_inv), full(tw), full(g_fwd), full(g_inv),
        ],
        out_specs=pl.BlockSpec((1, cs, seq), lambda j, b: (b, j, 0)),
        out_shape=jax.ShapeDtypeStruct((batch, c, seq), BF16),
        compiler_params=_cparams(("arbitrary", "arbitrary")),
        name="hyena",
    )(pT, pT, pT, par, khat, bd_fwd, bd_inv, tw, g_fwd, g_inv)


def _nat_kernel(q_ref, k_ref, v_ref, bias_ref, o_ref, *, rows, scale, rows_per_iter):
    npair = q_ref.shape[2] // LANES
    win = NAT_KH * GRID_W
    lo = lax.broadcasted_iota(jnp.int32, (GRID_W, LANES), 1) < LANES // 2

    def body(it, carry):
        chains = []
        for u in range(rows_per_iter):
            i = it * rows_per_iter + u
            rs = jnp.clip(i - NAT_KH // 2, 0, rows - NAT_KH)
            d = rs - i + (NAT_KH - 1)
            q0 = pl.multiple_of(i * GRID_W, GRID_W)
            k0 = pl.multiple_of(rs * GRID_W, GRID_W)
            for p in range(npair):
                chains.append((d, q0, k0, p, slice(p * LANES, (p + 1) * LANES)))
        scores = []
        for d, q0, k0, p, cols in chains:
            qp = q_ref[0, pl.ds(q0, GRID_W), cols].astype(F32) * scale
            q2 = jnp.concatenate([jnp.where(lo, qp, 0.0), jnp.where(lo, 0.0, qp)], axis=0)
            s = lax.dot_general(q2.astype(BF16), k_ref[0, pl.ds(k0, win), cols],
                                (((1,), (1,)), ((), ())), preferred_element_type=F32)
            scores.append(s + bias_ref[d, p])
        probs = []
        for s in scores:
            m = jnp.max(s, axis=-1, keepdims=True)
            pe = jnp.exp(s - m)
            probs.append((pe.astype(BF16), jnp.sum(pe, axis=-1, keepdims=True)))
        for (d, q0, k0, p, cols), (pe, l) in zip(chains, probs):
            o = jnp.dot(pe, v_ref[0, pl.ds(k0, win), cols], preferred_element_type=F32) / l
            o_ref[0, pl.ds(q0, GRID_W), cols] = jnp.where(
                lo, o[:GRID_W], o[GRID_W:]).astype(o_ref.dtype)
        return carry

    lax.fori_loop(0, rows // rows_per_iter, body, 0)


def _nat(pn3, bias, batch, seq, width, hd):
    gw = 2 * LANES
    ng = width // gw
    rows = seq // GRID_W
    blk = lambda k: pl.BlockSpec((1, seq, gw), lambda b, hg, k=k: (b, 0, k * ng + hg))
    return pl.pallas_call(
        functools.partial(_nat_kernel, rows=rows, scale=hd ** -0.5, rows_per_iter=4),
        grid=(batch, ng),
        in_specs=[
            blk(0), blk(1), blk(2),
            pl.BlockSpec((NAT_KH, gw // LANES, 2 * GRID_W, NAT_KH * GRID_W),
                         lambda b, hg: (0, hg, 0, 0)),
        ],
        out_specs=pl.BlockSpec((1, seq, gw), lambda b, hg: (b, 0, hg)),
        out_shape=jax.ShapeDtypeStruct((batch, seq, width), BF16),
        compiler_params=_cparams(("arbitrary", "arbitrary")),
        name="nat",
    )(pn3, pn3, pn3, bias)


def _nat_bias_table(rpb):
    h = rpb.shape[0]
    cols = np.arange(GRID_W)
    col_start = np.clip(cols - NAT_KW // 2, 0, GRID_W - NAT_KW)
    cc = cols[None, :]
    valid = (cc >= col_start[:, None]) & (cc < col_start[:, None] + NAT_KW)
    col_off = np.clip(cc - cols[:, None] + (NAT_KW - 1), 0, 2 * NAT_KW - 2)
    row_off = np.arange(NAT_KH)[:, None] + np.arange(NAT_KH)[None, :]
    t = rpb.astype(F32)[:, row_off]
    t = t[:, :, :, col_off]
    t = jnp.where(valid[None, None, None], t, MASK_VALUE)
    t = jnp.transpose(t, (1, 0, 3, 2, 4))
    return t.reshape(NAT_KH, h // 2, 2 * GRID_W, NAT_KH * GRID_W)


def _out_proj_kernel(x_ref, yh_ref, yn_ref, gh_ref, gn_ref, woh_ref, won_ref, h_ref):
    tm = x_ref.shape[0]
    yh = yh_ref[0].astype(F32)
    ms = jnp.mean(yh * yh, axis=0, keepdims=True)
    gh = jnp.concatenate([gh_ref[...]] * (tm // LANES), axis=1)
    nh = (yh * lax.rsqrt(ms + NORM_EPS) * gh).astype(BF16)
    nn = _rms(yn_ref[...].astype(F32), gn_ref[...]).astype(BF16)
    acc = lax.dot_general(nh, woh_ref[...], (((0,), (0,)), ((), ())),
                          preferred_element_type=F32)
    acc = acc + jnp.dot(nn, won_ref[...], preferred_element_type=F32)
    h_ref[...] = x_ref[...] + acc


def _out_proj(x2, yhT, yn, gh, gn, woh, won, seq, tm):
    ntok, d = x2.shape
    c = yn.shape[1]
    tpb = seq // tm
    full = lambda a: pl.BlockSpec(a.shape, lambda i: (0,) * a.ndim)
    return pl.pallas_call(
        _out_proj_kernel,
        grid=(ntok // tm,),
        in_specs=[
            pl.BlockSpec((tm, d), lambda i: (i, 0)),
            pl.BlockSpec((1, c, tm), lambda i: (i // tpb, 0, i % tpb)),
            pl.BlockSpec((tm, c), lambda i: (i, 0)),
            full(gh), full(gn), full(woh), full(won),
        ],
        out_specs=pl.BlockSpec((tm, d), lambda i: (i, 0)),
        out_shape=jax.ShapeDtypeStruct((ntok, d), F32),
        compiler_params=_cparams(("arbitrary",)),
        name="out_proj",
    )(x2, yhT, yn, gh, gn, woh, won)


def _mlp_kernel(h_ref, g_ref, wu_ref, wd_ref, gf_ref, o_ref, a_ref):
    j = pl.program_id(1)

    @pl.when(j == 0)
    def _():
        h = h_ref[...]
        a_ref[...] = _rms(h, g_ref[...]).astype(BF16)
        o_ref[...] = h

    u = jnp.dot(a_ref[...], wu_ref[...], preferred_element_type=F32)
    u = jnp.square(jnp.maximum(u, 0.0)).astype(BF16)
    o_ref[...] += jnp.dot(u, wd_ref[...], preferred_element_type=F32)

    @pl.when(j == pl.num_programs(1) - 1)
    def _():
        o_ref[...] = _rms(o_ref[...], gf_ref[...])


def _mlp(h, g, wu, wd, gf, tm, tf):
    ntok, d = h.shape
    dff = wu.shape[1]
    nt, nf = ntok // tm, dff // tf
    h_map = lambda i, j: (jnp.minimum(i + (j >= nf // 2).astype(jnp.int32), nt - 1), 0)
    return pl.pallas_call(
        _mlp_kernel,
        grid=(nt, nf),
        in_specs=[
            pl.BlockSpec((tm, d), h_map),
            pl.BlockSpec((1, d), lambda i, j: (0, 0)),
            pl.BlockSpec((d, tf), lambda i, j: (0, j)),
            pl.BlockSpec((tf, d), lambda i, j: (j, 0)),
            pl.BlockSpec((1, d), lambda i, j: (0, 0)),
        ],
        out_specs=pl.BlockSpec((tm, d), lambda i, j: (i, 0)),
        out_shape=jax.ShapeDtypeStruct((ntok, d), F32),
        scratch_shapes=[pltpu.VMEM((tm, d), BF16)],
        compiler_params=_cparams(("arbitrary", "arbitrary")),
        name="mlp",
    )(h, g, wu, wd, gf)


def _positional_features(seq, emb):
    bands = (emb - 1) // 2
    t = np.linspace(0.0, 1.0, seq)
    w_ang = (2.0 * np.pi / seq) * np.arange(seq)
    f = np.linspace(1e-4, bands - 1, bands)
    ang = w_ang[None, :] * f[:, None]
    z = np.concatenate([t[None, :], np.cos(ang), -np.sin(ang)], axis=0)
    pad = (-z.shape[0]) % SUBLANES
    z = np.concatenate([z, np.zeros((pad, seq))], axis=0)
    return jnp.asarray(z, F32), jnp.asarray(t[None, :], F32)


def _lane_rep(v):
    return jnp.broadcast_to(v.astype(F32)[..., None], v.shape + (LANES,))


def _trunk(x, w):
    batch, seq, d = x.shape
    c = w["hy_bias"].shape[-1]
    r = seq // LANES
    rows = seq // GRID_W
    assert seq % LANES == 0 and c % STRIP == 0
    assert rows >= NAT_KH and seq % GRID_W == 0
    consts = _fft_constants(r)
    cs = 64

    emb = w["hy_pe_w1"].shape[0]
    zT, tvec = _positional_features(seq, emb)
    w1T = jnp.pad(w["hy_pe_w1"].astype(F32).T, ((0, 0), (0, zT.shape[0] - emb)))
    col = lambda v: v.astype(F32)[:, None]
    max_decay = math.log(HY_DECAY_TARGET) / HY_FAST_DECAY
    min_decay = math.log(HY_DECAY_TARGET) / HY_SLOW_DECAY
    absd = jnp.asarray(np.abs(np.linspace(min_decay, max_decay, c)), F32)
    hT = _filter_mlp(zT, tvec, w1T, col(w["hy_pe_b1"]), w["hy_pe_w2"].astype(F32).T,
                     col(w["hy_pe_b2"]), w["hy_pe_w3"].astype(F32).T, col(w["hy_pe_b3"]),
                     col(w["hy_pe_freq"]), w["hy_pe_w4"].astype(F32).T, _lane_rep(absd),
                     tl=512)
    khat = _filter_fft(hT, consts, c, r, cs)

    x2 = x.reshape(batch * seq, d)
    w_in = w["w_in"]
    n_hy = 3 * c
    tc = 1024
    whT = w_in[:, :n_hy].T.astype(BF16).reshape(n_hy // tc, tc, d)
    wn = w_in[:, n_hy:].astype(BF16).reshape(d, -1, tc).transpose(1, 0, 2)
    pT, pn = _in_proj(x2, w["norm_mix_g"].astype(F32)[None, :], whT, wn, batch, seq, tm=1024)

    cw = w["hy_conv_w"].astype(F32).reshape(3, 3, c)
    cb = w["hy_conv_b"].astype(F32).reshape(3, c)
    par = [jnp.concatenate([cw[:, p], cb[p][None]], axis=0) for p in range(3)]
    par = _lane_rep(jnp.concatenate(par + [w["hy_bias"].astype(F32)], axis=0))
    yhT = _hyena(pT, par, khat, consts, batch, c, cs, strips_per_iter=4 if r <= 32 else 2)

    nat_w = pn.shape[1] // 3
    bias = _nat_bias_table(w["nat_rpb"])
    yn = _nat(pn.reshape(batch, seq, 3 * nat_w), bias, batch, seq, nat_w, nat_w // NAT_HEADS)
    yn = yn.reshape(batch * seq, nat_w)

    w_out = w["w_out"].astype(BF16)
    h = _out_proj(x2, yhT, yn, _lane_rep(w["gnorm_hy"]), w["gnorm_nat"].astype(F32)[None, :],
                  w_out[:c], w_out[c:], seq, tm=512)
    y = _mlp(h, w["norm_mlp_g"].astype(F32)[None, :], w["w_up"].astype(BF16),
             w["w_down"].astype(BF16), w["norm_f_g"].astype(F32)[None, :], tm=512, tf=1024)
    return y.reshape(batch, seq, d)


def kernel(x_prompt, x_sample, norm_mix_g, w_in, hy_conv_w, hy_conv_b, hy_pe_w1, hy_pe_b1,
           hy_pe_w2, hy_pe_b2, hy_pe_w3, hy_pe_b3, hy_pe_freq, hy_pe_w4, hy_bias, nat_rpb,
           gnorm_hy, gnorm_nat, w_out, norm_mlp_g, w_up, w_down, norm_f_g):
    assert norm_mix_g.shape[0] == 1, "single-layer trunk"
    w = dict(norm_mix_g=norm_mix_g[0], w_in=w_in[0], hy_conv_w=hy_conv_w[0],
             hy_conv_b=hy_conv_b[0], hy_pe_w1=hy_pe_w1[0], hy_pe_b1=hy_pe_b1[0],
             hy_pe_w2=hy_pe_w2[0], hy_pe_b2=hy_pe_b2[0], hy_pe_w3=hy_pe_w3[0],
             hy_pe_b3=hy_pe_b3[0], hy_pe_freq=hy_pe_freq[0], hy_pe_w4=hy_pe_w4[0],
             hy_bias=hy_bias[0], nat_rpb=nat_rpb[0], gnorm_hy=gnorm_hy[0],
             gnorm_nat=gnorm_nat[0], w_out=w_out[0], norm_mlp_g=norm_mlp_g[0],
             w_up=w_up[0], w_down=w_down[0], norm_f_g=norm_f_g)
    return (_trunk(x_prompt, w), _trunk(x_sample, w))
```

```python
import functools
import math

import numpy as np
import jax
import jax.numpy as jnp
from jax import lax
from jax.experimental import pallas as pl
from jax.experimental.pallas import tpu as pltpu

F32 = jnp.float32
BF16 = jnp.bfloat16

LANES = 128
SUBLANES = 8
STRIP = 2 * SUBLANES
VMEM_LIMIT = 58 * 1024 * 1024

NORM_EPS = 1e-5
GRID_W = 64
NAT_HEADS = 16
NAT_KH = 8
NAT_KW = 16
HY_ORDER = 2
HY_FAST_DECAY = 0.3
HY_SLOW_DECAY = 1.5
HY_DECAY_TARGET = 1e-2
MASK_VALUE = -1e30


def _cparams(sem):
    return pltpu.CompilerParams(dimension_semantics=sem, vmem_limit_bytes=VMEM_LIMIT)


def _rms(x, g):
    ms = jnp.mean(x * x, axis=-1, keepdims=True)
    return x * lax.rsqrt(ms + NORM_EPS) * g


def _in_proj_kernel(xa_ref, xb_ref, g_ref, whT_ref, wn_ref, pT_ref, pn_ref, a_ref, *, n_hy):
    i, j = pl.program_id(0), pl.program_id(1)
    slot = i % 2

    def normalise(dst):
        half = xa_ref.shape[1]
        xa, xb = xa_ref[...], xb_ref[...]
        ss = jnp.sum(xa * xa, axis=-1, keepdims=True) + jnp.sum(xb * xb, axis=-1, keepdims=True)
        inv = lax.rsqrt(ss / (2 * half) + NORM_EPS)
        a_ref[dst, :, :half] = (xa * inv * g_ref[:, :half]).astype(BF16)
        a_ref[dst, :, half:] = (xb * inv * g_ref[:, half:]).astype(BF16)

    @pl.when((i == 0) & (j == 0))
    def _():
        normalise(0)

    @pl.when(j < n_hy)
    def _():
        pT_ref[0] = lax.dot_general(
            whT_ref[0], a_ref[slot], (((1,), (1,)), ((), ())),
            preferred_element_type=F32).astype(BF16)

    def nat_part():
        pn_ref[...] = jnp.dot(a_ref[slot], wn_ref[0],
                              preferred_element_type=F32).astype(BF16)

    @pl.when(j == n_hy)
    def _():
        nat_part()
        normalise(1 - slot)

    @pl.when(j > n_hy)
    def _():
        nat_part()


def _in_proj(x2, g, whT, wn, batch, seq, tm):
    ntok, d = x2.shape
    n_hy, tc, _ = whT.shape
    n_nat = wn.shape[0]
    tpb = seq // tm
    nt = ntok // tm
    assert n_hy >= 2
    ahead = lambda i, j, at: jnp.minimum(i + (j >= at).astype(jnp.int32), nt - 1)
    return pl.pallas_call(
        functools.partial(_in_proj_kernel, n_hy=n_hy),
        grid=(nt, n_hy + n_nat),
        in_specs=[
            pl.BlockSpec((tm, d // 2), lambda i, j: (ahead(i, j, 1), 0)),
            pl.BlockSpec((tm, d // 2), lambda i, j: (ahead(i, j, 2), 1)),
            pl.BlockSpec((1, d), lambda i, j: (0, 0)),
            pl.BlockSpec((1, tc, d), lambda i, j: (jnp.minimum(j, n_hy - 1), 0, 0)),
            pl.BlockSpec((1, d, tc), lambda i, j: (jnp.maximum(j - n_hy, 0), 0, 0)),
        ],
        out_specs=[
            pl.BlockSpec((1, tc, tm),
                         lambda i, j: (i // tpb, jnp.minimum(j, n_hy - 1), i % tpb)),
            pl.BlockSpec((tm, tc), lambda i, j: (i, jnp.maximum(j - n_hy, 0))),
        ],
        out_shape=[
            jax.ShapeDtypeStruct((batch, n_hy * tc, seq), BF16),
            jax.ShapeDtypeStruct((ntok, n_nat * tc), BF16),
        ],
        scratch_shapes=[pltpu.VMEM((2, tm, d), BF16)],
        compiler_params=_cparams(("arbitrary", "arbitrary")),
        name="in_proj",
    )(x2, x2, g, whT, wn)


def _fft_constants(r):
    n2f = 2 * r
    m = LANES * n2f
    k2 = np.arange(r, dtype=np.float64)[:, None]
    n2 = np.arange(r, dtype=np.float64)[None, :]
    th = 2.0 * np.pi * n2 * (k2 + 0.5) / n2f
    er, ei = np.cos(th), -np.sin(th)
    eye = np.eye(SUBLANES)
    bd_fwd = np.concatenate([np.kron(er, eye), np.kron(ei, eye)], axis=0)
    bd_inv = np.concatenate([np.kron(er.T, eye), np.kron(ei.T, eye)], axis=1) * (2.0 / m)
    n1 = np.arange(LANES, dtype=np.float64)[None, :]
    ph = 2.0 * np.pi * n1 * (k2 + 0.5) / m
    tw = np.stack([np.repeat(np.cos(ph), SUBLANES, axis=0),
                   np.repeat(-np.sin(ph), SUBLANES, axis=0)])
    a = np.arange(LANES, dtype=np.float64)
    ps = 2.0 * np.pi * np.outer(a, a) / LANES
    gr, gi = np.cos(ps), -np.sin(ps)
    g_fwd = np.block([[gr, gi], [-gi, gr]])
    g_inv = np.block([[gr, -gi], [gi, gr]])
    as_bf = lambda v: jnp.asarray(v, F32).astype(BF16)
    return as_bf(bd_fwd), as_bf(bd_inv), as_bf(tw), as_bf(g_fwd), as_bf(g_inv)


def _strip_to_pair(s):
    r = s.shape[1] // LANES
    return jnp.concatenate(
        [jnp.concatenate([s[:SUBLANES, n * LANES:(n + 1) * LANES],
                          s[SUBLANES:, n * LANES:(n + 1) * LANES]], axis=1)
         for n in range(r)], axis=0)


def _pair_to_strip(z):
    r = z.shape[0] // SUBLANES
    return jnp.concatenate(
        [jnp.concatenate([z[n * SUBLANES:(n + 1) * SUBLANES, :LANES],
                          z[n * SUBLANES:(n + 1) * SUBLANES, LANES:]], axis=0)
         for n in range(r)], axis=1)


def _fft_fwd(zs, bd_fwd, tr, ti, g_fwd):
    tr_rows = zs[0].shape[0]
    tr2 = jnp.concatenate([tr, tr], axis=1)
    ti2 = jnp.concatenate([ti, ti], axis=1)
    a_s = [jnp.dot(bd_fwd, z.astype(BF16), preferred_element_type=F32) for z in zs]
    b_s = []
    for a in a_s:
        a = a.astype(BF16)
        ar, ai = a[:tr_rows], a[tr_rows:]
        br = ar * tr2 - ai * ti2
        bi = ar * ti2 + ai * tr2
        b_s.append(jnp.concatenate([
            jnp.concatenate([br[:, :LANES], bi[:, :LANES]], axis=1),
            jnp.concatenate([br[:, LANES:], bi[:, LANES:]], axis=1)], axis=0))
    return [jnp.dot(b, g_fwd, preferred_element_type=F32) for b in b_s]


def _cmul_spectrum(x, kr, ki):
    x = x.astype(BF16)
    xr, xi = x[:, :LANES], x[:, LANES:]
    return jnp.concatenate([xr * kr - xi * ki, xr * ki + xi * kr], axis=1)


def _fft_inv(ys, bd_inv, tr, ti, g_inv):
    tr_rows = ys[0].shape[0] // 2
    trs = jnp.concatenate([tr, tr], axis=0)
    tis = jnp.concatenate([ti, ti], axis=0)
    p_s = [jnp.dot(y, g_inv, preferred_element_type=F32) for y in ys]
    q_s = []
    for p in p_s:
        p = p.astype(BF16)
        pr, pi = p[:, :LANES], p[:, LANES:]
        qr = pr * trs + pi * tis
        qi = pi * trs - pr * tis
        q_s.append(jnp.concatenate([
            jnp.concatenate([qr[:tr_rows], qr[tr_rows:]], axis=1),
            jnp.concatenate([qi[:tr_rows], qi[tr_rows:]], axis=1)], axis=0))
    return [jnp.dot(bd_inv, q, preferred_element_type=F32) for q in q_s]


def _filter_mlp_kernel(zT_ref, t_ref, w1_ref, b1_ref, w2_ref, b2_ref, w3_ref, b3_ref,
                       fr_ref, w4_ref, ad_ref, o_ref, *, chunk):
    hp = lax.Precision.HIGHEST
    tl = zT_ref.shape[1]
    fr = fr_ref[...]
    h = jnp.sin(fr * (jnp.dot(w1_ref[...], zT_ref[...], precision=hp,
                              preferred_element_type=F32) + b1_ref[...]))
    h = jnp.sin(fr * (jnp.dot(w2_ref[...], h, precision=hp,
                              preferred_element_type=F32) + b2_ref[...]))
    h = jnp.sin(fr * (jnp.dot(w3_ref[...], h, precision=hp,
                              preferred_element_type=F32) + b3_ref[...]))
    h_hi = h.astype(BF16)
    h_lo = (h - h_hi.astype(F32)).astype(BF16)
    h3 = jnp.concatenate([h_hi, h_lo, h_hi], axis=0)
    t = t_ref[...]
    c = ad_ref.shape[0]
    first = (pl.program_id(0) == 0) & (lax.broadcasted_iota(jnp.int32, (chunk, tl), 1) == 0)
    for cc in range(c // chunk):
        ad = jnp.concatenate([ad_ref[cc * chunk:(cc + 1) * chunk, :]] * (tl // LANES), axis=1)
        decay = jnp.exp(-(t * ad))
        for od in range(2 * HY_ORDER):
            r0 = od * c + cc * chunk
            v = jnp.dot(w4_ref[r0:r0 + chunk, :], h3, preferred_element_type=F32) * decay
            if od % 2 == 1:
                v = jnp.where(first, 0.0, v)
            o_ref[r0:r0 + chunk, :] = v


def _filter_mlp(zT, tvec, w1T, b1, w2T, b2, w3T, b3, fr, w4T, ad, tl):
    nz, seq = zT.shape
    rows, hid = w4T.shape
    full = lambda a: pl.BlockSpec(a.shape, lambda i: (0,) * a.ndim)
    return pl.pallas_call(
        functools.partial(_filter_mlp_kernel, chunk=512),
        grid=(seq // tl,),
        in_specs=[
            pl.BlockSpec((nz, tl), lambda i: (0, i)),
            pl.BlockSpec((1, tl), lambda i: (0, i)),
            full(w1T), full(b1), full(w2T), full(b2), full(w3T), full(b3), full(fr),
            full(w4T), full(ad),
        ],
        out_specs=pl.BlockSpec((rows, tl), lambda i: (0, i)),
        out_shape=jax.ShapeDtypeStruct((rows, seq), F32),
        compiler_params=_cparams(("arbitrary",)),
        name="filter_mlp",
    )(zT, tvec, w1T, b1, w2T, b2, w3T, b3, fr, w4T, ad)


def _filter_fft_kernel(hf_ref, hb_ref, bdf_ref, tw_ref, gf_ref, o_ref):
    nstrips = hf_ref.shape[0] // STRIP

    def body(s, carry):
        c0 = pl.multiple_of(s * STRIP, STRIP)
        xf, xb = _fft_fwd(
            [_strip_to_pair(ref[pl.ds(c0, STRIP), :]) for ref in (hf_ref, hb_ref)],
            bdf_ref[...], tw_ref[0], tw_ref[1], gf_ref[...])
        o_ref[0, s] = jnp.concatenate(
            [xf[:, :LANES] + xb[:, :LANES], xf[:, LANES:] - xb[:, LANES:]],
            axis=1).astype(o_ref.dtype)
        return carry

    lax.fori_loop(0, nstrips, body, 0)


def _filter_fft(hT, consts, c, r, cs):
    bd_fwd, _, tw, g_fwd, _ = consts
    seq = hT.shape[1]
    nblk = c // cs
    full = lambda a: pl.BlockSpec(a.shape, lambda o, j: (0,) * a.ndim)
    return pl.pallas_call(
        _filter_fft_kernel,
        grid=(HY_ORDER, nblk),
        in_specs=[
            pl.BlockSpec((cs, seq), lambda o, j: ((2 * o) * nblk + j, 0)),
            pl.BlockSpec((cs, seq), lambda o, j: ((2 * o + 1) * nblk + j, 0)),
            full(bd_fwd), full(tw), full(g_fwd),
        ],
        out_specs=pl.BlockSpec((1, cs // STRIP, STRIP * r, 2 * LANES),
                               lambda o, j: (o, j, 0, 0)),
        out_shape=jax.ShapeDtypeStruct((HY_ORDER, c // STRIP, STRIP * r, 2 * LANES), BF16),
        compiler_params=_cparams(("arbitrary", "arbitrary")),
        name="filter_fft",
    )(hT, hT, bd_fwd, tw, g_fwd)


def _hyena_kernel(v_ref, x1_ref, x2_ref, par_ref, kh_ref, bdf_ref, bdi_ref, tw_ref, gf_ref,
                  gi_ref, o_ref, *, strips_per_iter):
    seq = v_ref.shape[2]
    r = seq // LANES
    nstrips = v_ref.shape[1] // STRIP
    lane = lax.broadcasted_iota(jnp.int32, (STRIP, LANES), 1)

    def short_conv(p, w):
        prev = pltpu.roll(p, 1, axis=1)
        nxt = pltpu.roll(p, seq - 1, axis=1)
        prev = jnp.concatenate(
            [jnp.where(lane == 0, 0.0, prev[:, :LANES]), prev[:, LANES:]], axis=1)
        nxt = jnp.concatenate(
            [nxt[:, :seq - LANES], jnp.where(lane == LANES - 1, 0.0, nxt[:, seq - LANES:])],
            axis=1)
        wide = lambda k: jnp.concatenate([w[k]] * r, axis=1)
        return prev * wide(0) + p * wide(1) + nxt * wide(2) + wide(3)

    def body(it, carry):
        strips = [it * strips_per_iter + k for k in range(strips_per_iter)]
        rows = [pl.ds(pl.multiple_of(s * STRIP, STRIP), STRIP) for s in strips]
        tr, ti = tw_ref[0], tw_ref[1]
        part = lambda ref, rw, k0: _strip_to_pair(short_conv(
            ref[0, rw, :].astype(F32), [par_ref[k0 + k, rw, :] for k in range(4)]))
        zs = [part(v_ref, rw, 0) for rw in rows]
        gates = (x1_ref, x2_ref)
        for o in range(HY_ORDER):
            gs = [part(gates[o], rw, 4 + 4 * o) for rw in rows]
            xs = _fft_fwd(zs, bdf_ref[...], tr, ti, gf_ref[...])
            ys = [_cmul_spectrum(x, kh_ref[o, s, :, :LANES], kh_ref[o, s, :, LANES:])
                  for s, x in zip(strips, xs)]
            convs = _fft_inv(ys, bdi_ref[...], tr, ti, gi_ref[...])
            nxt = []
            for rw, z, gate, conv in zip(rows, zs, gs, convs):
                b = par_ref[12 + o, rw, :]
                bias = jnp.concatenate(
                    [jnp.concatenate([b[:SUBLANES], b[SUBLANES:]], axis=1)] * r, axis=0)
                nxt.append(gate * (conv + z * bias))
            zs = nxt
        for rw, z in zip(rows, zs):
            o_ref[0, rw, :] = _pair_to_strip(z).astype(o_ref.dtype)
        return carry

    lax.fori_loop(0, nstrips // strips_per_iter, body, 0)


def _hyena(pT, par, khat, consts, batch, c, cs, strips_per_iter):
    bd_fwd, bd_inv, tw, g_fwd, g_inv = consts
    seq = pT.shape[2]
    nblk = c // cs
    full = lambda a: pl.BlockSpec(a.shape, lambda j, b: (0,) * a.ndim)
    part = lambda k: pl.BlockSpec((1, cs, seq), lambda j, b, k=k: (b, k * nblk + j, 0))
    return pl.pallas_call(
        functools.partial(_hyena_kernel, strips_per_iter=strips_per_iter),
        grid=(nblk, batch),
        in_specs=[
            part(0), part(1), part(2),
            pl.BlockSpec((par.shape[0], cs, LANES), lambda j, b: (0, j, 0)),
            pl.BlockSpec((HY_ORDER, cs // STRIP) + khat.shape[2:], lambda j, b: (0, j, 0, 0)),
            full(bd_fwd), full(bd_inv), full(tw), full(g_fwd), full(g_inv),
        ],
        out_specs=pl.BlockSpec((1, cs, seq), lambda j, b: (b, j, 0)),
        out_shape=jax.ShapeDtypeStruct((batch, c, seq), BF16),
        compiler_params=_cparams(("arbitrary", "arbitrary")),
        name="hyena",
    )(pT, pT, pT, par, khat, bd_fwd, bd_inv, tw, g_fwd, g_inv)


def _nat_kernel(q_ref, k_ref, v_ref, bias_ref, o_ref, *, rows, scale, rows_per_iter):
    npair = q_ref.shape[2] // LANES
    win = NAT_KH * GRID_W
    lo = lax.broadcasted_iota(jnp.int32, (GRID_W, LANES), 1) < LANES // 2

    def body(it, carry):
        chains = []
        for u in range(rows_per_iter):
            i = it * rows_per_iter + u
            rs = jnp.clip(i - NAT_KH // 2, 0, rows - NAT_KH)
            d = rs - i + (NAT_KH - 1)
            q0 = pl.multiple_of(i * GRID_W, GRID_W)
            k0 = pl.multiple_of(rs * GRID_W, GRID_W)
            for p in range(npair):
                chains.append((d, q0, k0, p, slice(p * LANES, (p + 1) * LANES)))
        scores = []
        for d, q0, k0, p, cols in chains:
            qp = q_ref[0, pl.ds(q0, GRID_W), cols].astype(F32) * scale
            q2 = jnp.concatenate([jnp.where(lo, qp, 0.0), jnp.where(lo, 0.0, qp)], axis=0)
            s = lax.dot_general(q2.astype(BF16), k_ref[0, pl.ds(k0, win), cols],
                                (((1,), (1,)), ((), ())), preferred_element_type=F32)
            scores.append(s + bias_ref[d, p])
        probs = []
        for s in scores:
            m = jnp.max(s, axis=-1, keepdims=True)
            pe = jnp.exp((s - m).astype(BF16))
            probs.append((pe, jnp.sum(pe.astype(F32), axis=-1, keepdims=True)))
        for (d, q0, k0, p, cols), (pe, l) in zip(chains, probs):
            o = jnp.dot(pe, v_ref[0, pl.ds(k0, win), cols], preferred_element_type=F32) / l
            o_ref[0, pl.ds(q0, GRID_W), cols] = jnp.where(
                lo, o[:GRID_W], o[GRID_W:]).astype(o_ref.dtype)
        return carry

    lax.fori_loop(0, rows // rows_per_iter, body, 0)


def _nat(pn3, bias, batch, seq, width, hd):
    gw = 2 * LANES
    ng = width // gw
    rows = seq // GRID_W
    blk = lambda k: pl.BlockSpec((1, seq, gw), lambda b, hg, k=k: (b, 0, k * ng + hg))
    return pl.pallas_call(
        functools.partial(_nat_kernel, rows=rows, scale=hd ** -0.5, rows_per_iter=4),
        grid=(batch, ng),
        in_specs=[
            blk(0), blk(1), blk(2),
            pl.BlockSpec((NAT_KH, gw // LANES, 2 * GRID_W, NAT_KH * GRID_W),
                         lambda b, hg: (0, hg, 0, 0)),
        ],
        out_specs=pl.BlockSpec((1, seq, gw), lambda b, hg: (b, 0, hg)),
        out_shape=jax.ShapeDtypeStruct((batch, seq, width), BF16),
        compiler_params=_cparams(("arbitrary", "arbitrary")),
        name="nat",
    )(pn3, pn3, pn3, bias)


def _nat_bias_table(rpb):
    h = rpb.shape[0]
    cols = np.arange(GRID_W)
    col_start = np.clip(cols - NAT_KW // 2, 0, GRID_W - NAT_KW)
    cc = cols[None, :]
    valid = (cc >= col_start[:, None]) & (cc < col_start[:, None] + NAT_KW)
    col_off = np.clip(cc - cols[:, None] + (NAT_KW - 1), 0, 2 * NAT_KW - 2)
    row_off = np.arange(NAT_KH)[:, None] + np.arange(NAT_KH)[None, :]
    t = rpb.astype(F32)[:, row_off]
    t = t[:, :, :, col_off]
    t = jnp.where(valid[None, None, None], t, MASK_VALUE)
    t = jnp.transpose(t, (1, 0, 3, 2, 4))
    return t.reshape(NAT_KH, h // 2, 2 * GRID_W, NAT_KH * GRID_W)


def _out_proj_kernel(x_ref, yh_ref, yn_ref, gh_ref, gn_ref, woh_ref, won_ref, h_ref):
    tm = x_ref.shape[0]
    yh = yh_ref[0].astype(F32)
    ms = jnp.mean(yh * yh, axis=0, keepdims=True)
    gh = jnp.concatenate([gh_ref[...]] * (tm // LANES), axis=1)
    nh = (yh * lax.rsqrt(ms + NORM_EPS) * gh).astype(BF16)
    nn = _rms(yn_ref[...].astype(F32), gn_ref[...]).astype(BF16)
    acc = lax.dot_general(nh, woh_ref[...], (((0,), (0,)), ((), ())),
                          preferred_element_type=F32)
    acc = acc + jnp.dot(nn, won_ref[...], preferred_element_type=F32)
    h_ref[...] = x_ref[...] + acc


def _out_proj(x2, yhT, yn, gh, gn, woh, won, seq, tm):
    ntok, d = x2.shape
    c = yn.shape[1]
    tpb = seq // tm
    full = lambda a: pl.BlockSpec(a.shape, lambda i: (0,) * a.ndim)
    return pl.pallas_call(
        _out_proj_kernel,
        grid=(ntok // tm,),
        in_specs=[
            pl.BlockSpec((tm, d), lambda i: (i, 0)),
            pl.BlockSpec((1, c, tm), lambda i: (i // tpb, 0, i % tpb)),
            pl.BlockSpec((tm, c), lambda i: (i, 0)),
            full(gh), full(gn), full(woh), full(won),
        ],
        out_specs=pl.BlockSpec((tm, d), lambda i: (i, 0)),
        out_shape=jax.ShapeDtypeStruct((ntok, d), F32),
        compiler_params=_cparams(("arbitrary",)),
        name="out_proj",
    )(x2, yhT, yn, gh, gn, woh, won)


def _mlp_kernel(h_ref, g_ref, wu_ref, wd_ref, gf_ref, o_ref, a_ref):
    i, j = pl.program_id(0), pl.program_id(1)
    last = pl.num_programs(1) - 1
    slot = i % 2

    @pl.when((i == 0) & (j == 0))
    def _():
        a_ref[0] = _rms(h_ref[...], g_ref[...]).astype(BF16)

    @pl.when(j == 0)
    def _():
        o_ref[...] = h_ref[...]

    def hidden_tile():
        u = jnp.dot(a_ref[slot], wu_ref[...], preferred_element_type=F32)
        u = jnp.square(jnp.maximum(u, 0.0)).astype(BF16)
        return o_ref[...] + jnp.dot(u, wd_ref[...], preferred_element_type=F32)

    @pl.when(j < last)
    def _():
        o_ref[...] = hidden_tile()

    @pl.when(j == last)
    def _():
        acc = hidden_tile()
        a_ref[1 - slot] = _rms(h_ref[...], g_ref[...]).astype(BF16)
        o_ref[...] = _rms(acc, gf_ref[...])


def _mlp(h, g, wu, wd, gf, tm, tf):
    ntok, d = h.shape
    dff = wu.shape[1]
    nt, nf = ntok // tm, dff // tf
    assert nf >= 2
    h_map = lambda i, j: (jnp.minimum(i + (j >= nf // 2).astype(jnp.int32), nt - 1), 0)
    return pl.pallas_call(
        _mlp_kernel,
        grid=(nt, nf),
        in_specs=[
            pl.BlockSpec((tm, d), h_map),
            pl.BlockSpec((1, d), lambda i, j: (0, 0)),
            pl.BlockSpec((d, tf), lambda i, j: (0, j)),
            pl.BlockSpec((tf, d), lambda i, j: (j, 0)),
            pl.BlockSpec((1, d), lambda i, j: (0, 0)),
        ],
        out_specs=pl.BlockSpec((tm, d), lambda i, j: (i, 0)),
        out_shape=jax.ShapeDtypeStruct((ntok, d), F32),
        scratch_shapes=[pltpu.VMEM((2, tm, d), BF16)],
        compiler_params=_cparams(("arbitrary", "arbitrary")),
        name="mlp",
    )(h, g, wu, wd, gf)


def _positional_features(seq, emb):
    bands = (emb - 1) // 2
    t = np.linspace(0.0, 1.0, seq)
    w_ang = (2.0 * np.pi / seq) * np.arange(seq)
    f = np.linspace(1e-4, bands - 1, bands)
    ang = w_ang[None, :] * f[:, None]
    z = np.concatenate([t[None, :], np.cos(ang), -np.sin(ang)], axis=0)
    pad = (-z.shape[0]) % SUBLANES
    z = np.concatenate([z, np.zeros((pad, seq))], axis=0)
    return jnp.asarray(z, F32), jnp.asarray(t[None, :], F32)


def _lane_rep(v):
    return jnp.broadcast_to(v.astype(F32)[..., None], v.shape + (LANES,))


def _trunk(x, w):
    batch, seq, d = x.shape
    c = w["hy_bias"].shape[-1]
    r = seq // LANES
    rows = seq // GRID_W
    assert seq % LANES == 0 and c % STRIP == 0
    assert rows >= NAT_KH and seq % GRID_W == 0
    consts = _fft_constants(r)
    cs = 64

    emb = w["hy_pe_w1"].shape[0]
    zT, tvec = _positional_features(seq, emb)
    w1T = jnp.pad(w["hy_pe_w1"].astype(F32).T, ((0, 0), (0, zT.shape[0] - emb)))
    col = lambda v: v.astype(F32)[:, None]
    max_decay = math.log(HY_DECAY_TARGET) / HY_FAST_DECAY
    min_decay = math.log(HY_DECAY_TARGET) / HY_SLOW_DECAY
    absd = jnp.asarray(np.abs(np.linspace(min_decay, max_decay, c)), F32)
    w4T = w["hy_pe_w4"].astype(F32).T
    hi_bits = lax.bitcast_convert_type(w4T, jnp.uint32) & jnp.uint32(0xFFFF0000)
    w4_hi_f32 = lax.bitcast_convert_type(hi_bits, F32)
    w4_hi = w4_hi_f32.astype(BF16)
    w4_lo = (w4T - w4_hi_f32).astype(BF16)
    hT = _filter_mlp(zT, tvec, w1T, col(w["hy_pe_b1"]), w["hy_pe_w2"].astype(F32).T,
                     col(w["hy_pe_b2"]), w["hy_pe_w3"].astype(F32).T, col(w["hy_pe_b3"]),
                     col(w["hy_pe_freq"]), jnp.concatenate([w4_hi, w4_hi, w4_lo], axis=1),
                     _lane_rep(absd), tl=512)
    khat = _filter_fft(hT, consts, c, r, cs)

    x2 = x.reshape(batch * seq, d)
    w_in = w["w_in"]
    n_hy = 3 * c
    tc = 1024
    whT = w_in[:, :n_hy].T.astype(BF16).reshape(n_hy // tc, tc, d)
    wn = w_in[:, n_hy:].astype(BF16).reshape(d, -1, tc).transpose(1, 0, 2)
    pT, pn = _in_proj(x2, w["norm_mix_g"].astype(F32)[None, :], whT, wn, batch, seq, tm=1024)

    cw = w["hy_conv_w"].astype(F32).reshape(3, 3, c)
    cb = w["hy_conv_b"].astype(F32).reshape(3, c)
    par = [jnp.concatenate([cw[:, p], cb[p][None]], axis=0) for p in range(3)]
    par = _lane_rep(jnp.concatenate(par + [w["hy_bias"].astype(F32)], axis=0))
    yhT = _hyena(pT, par, khat, consts, batch, c, cs, strips_per_iter=4 if r <= 32 else 2)

    nat_w = pn.shape[1] // 3
    bias = _nat_bias_table(w["nat_rpb"])
    yn = _nat(pn.reshape(batch, seq, 3 * nat_w), bias, batch, seq, nat_w, nat_w // NAT_HEADS)
    yn = yn.reshape(batch * seq, nat_w)

    w_out = w["w_out"].astype(BF16)
    h = _out_proj(x2, yhT, yn, _lane_rep(w["gnorm_hy"]), w["gnorm_nat"].astype(F32)[None, :],
                  w_out[:c], w_out[c:], seq, tm=512)
    y = _mlp(h, w["norm_mlp_g"].astype(F32)[None, :], w["w_up"].astype(BF16),
             w["w_down"].astype(BF16), w["norm_f_g"].astype(F32)[None, :], tm=512, tf=2048)
    return y.reshape(batch, seq, d)


def kernel(x_prompt, x_sample, norm_mix_g, w_in, hy_conv_w, hy_conv_b, hy_pe_w1, hy_pe_b1,
           hy_pe_w2, hy_pe_b2, hy_pe_w3, hy_pe_b3, hy_pe_freq, hy_pe_w4, hy_bias, nat_rpb,
           gnorm_hy, gnorm_nat, w_out, norm_mlp_g, w_up, w_down, norm_f_g):
    assert norm_mix_g.shape[0] == 1, "single-layer trunk"
    w = dict(norm_mix_g=norm_mix_g[0], w_in=w_in[0], hy_conv_w=hy_conv_w[0],
             hy_conv_b=hy_conv_b[0], hy_pe_w1=hy_pe_w1[0], hy_pe_b1=hy_pe_b1[0],
             hy_pe_w2=hy_pe_w2[0], hy_pe_b2=hy_pe_b2[0], hy_pe_w3=hy_pe_w3[0],
             hy_pe_b3=hy_pe_b3[0], hy_pe_freq=hy_pe_freq[0], hy_pe_w4=hy_pe_w4[0],
             hy_bias=hy_bias[0], nat_rpb=nat_rpb[0], gnorm_hy=gnorm_hy[0],
             gnorm_nat=gnorm_nat[0], w_out=w_out[0], norm_mlp_g=norm_mlp_g[0],
             w_up=w_up[0], w_down=w_down[0], norm_f_g=norm_f_g)
    return (_trunk(x_prompt, w), _trunk(x_sample, w))
```

```python
import functools
import math

import numpy as np
import jax
import jax.numpy as jnp
from jax import lax
from jax.experimental import pallas as pl
from jax.experimental.pallas import tpu as pltpu

F32 = jnp.float32
BF16 = jnp.bfloat16

LANES = 128
SUBLANES = 8
STRIP = 2 * SUBLANES
VMEM_LIMIT = 58 * 1024 * 1024

NORM_EPS = 1e-5
GRID_W = 64
NAT_HEADS = 16
NAT_KH = 8
NAT_KW = 16
HY_ORDER = 2
HY_FAST_DECAY = 0.3
HY_SLOW_DECAY = 1.5
HY_DECAY_TARGET = 1e-2
MASK_VALUE = -1e30


def _cparams(sem):
    return pltpu.CompilerParams(dimension_semantics=sem, vmem_limit_bytes=VMEM_LIMIT)


def _rms(x, g):
    ms = jnp.mean(x * x, axis=-1, keepdims=True)
    return x * lax.rsqrt(ms + NORM_EPS) * g


def _in_proj_kernel(xa_ref, xb_ref, g_ref, whT_ref, wn_ref, pT_ref, pn_ref, a_ref, *, n_hy):
    i, j = pl.program_id(0), pl.program_id(1)
    slot = i % 2

    def normalise(dst):
        half = xa_ref.shape[1]
        xa, xb = xa_ref[...], xb_ref[...]
        ss = jnp.sum(xa * xa, axis=-1, keepdims=True) + jnp.sum(xb * xb, axis=-1, keepdims=True)
        inv = lax.rsqrt(ss / (2 * half) + NORM_EPS)
        a_ref[dst, :, :half] = (xa * inv * g_ref[:, :half]).astype(BF16)
        a_ref[dst, :, half:] = (xb * inv * g_ref[:, half:]).astype(BF16)

    @pl.when((i == 0) & (j == 0))
    def _():
        normalise(0)

    @pl.when(j < n_hy)
    def _():
        pT_ref[0] = lax.dot_general(
            whT_ref[0], a_ref[slot], (((1,), (1,)), ((), ())),
            preferred_element_type=F32).astype(BF16)

    def nat_part():
        pn_ref[...] = jnp.dot(a_ref[slot], wn_ref[...],
                              preferred_element_type=F32).astype(BF16)

    @pl.when(j == n_hy)
    def _():
        nat_part()
        normalise(1 - slot)

    @pl.when(j > n_hy)
    def _():
        nat_part()


def _in_proj(x2, g, whT, wn, batch, seq, tm):
    ntok, d = x2.shape
    n_hy, tc, _ = whT.shape
    n_nat = wn.shape[1] // tc
    tpb = seq // tm
    nt = ntok // tm
    assert n_hy >= 2
    ahead = lambda i, j, at: jnp.minimum(i + (j >= at).astype(jnp.int32), nt - 1)
    return pl.pallas_call(
        functools.partial(_in_proj_kernel, n_hy=n_hy),
        grid=(nt, n_hy + n_nat),
        in_specs=[
            pl.BlockSpec((tm, d // 2), lambda i, j: (ahead(i, j, 1), 0)),
            pl.BlockSpec((tm, d // 2), lambda i, j: (ahead(i, j, 2), 1)),
            pl.BlockSpec((1, d), lambda i, j: (0, 0)),
            pl.BlockSpec((1, tc, d), lambda i, j: (jnp.minimum(j, n_hy - 1), 0, 0)),
            pl.BlockSpec((d, tc), lambda i, j: (0, jnp.maximum(j - n_hy, 0))),
        ],
        out_specs=[
            pl.BlockSpec((1, tc, tm),
                         lambda i, j: (i // tpb, jnp.minimum(j, n_hy - 1), i % tpb)),
            pl.BlockSpec((tm, tc), lambda i, j: (i, jnp.maximum(j - n_hy, 0))),
        ],
        out_shape=[
            jax.ShapeDtypeStruct((batch, n_hy * tc, seq), BF16),
            jax.ShapeDtypeStruct((ntok, n_nat * tc), BF16),
        ],
        scratch_shapes=[pltpu.VMEM((2, tm, d), BF16)],
        compiler_params=_cparams(("arbitrary", "arbitrary")),
        name="in_proj",
    )(x2, x2, g, whT, wn)


def _fft_constants(r):
    n2f = 2 * r
    m = LANES * n2f
    k2 = np.arange(r, dtype=np.float64)[:, None]
    n2 = np.arange(r, dtype=np.float64)[None, :]
    th = 2.0 * np.pi * n2 * (k2 + 0.5) / n2f
    er, ei = np.cos(th), -np.sin(th)
    eye = np.eye(SUBLANES)
    bd_fwd = np.concatenate([np.kron(er, eye), np.kron(ei, eye)], axis=0)
    bd_inv = np.concatenate([np.kron(er.T, eye), np.kron(ei.T, eye)], axis=1) * (2.0 / m)
    n1 = np.arange(LANES, dtype=np.float64)[None, :]
    ph = 2.0 * np.pi * n1 * (k2 + 0.5) / m
    tw = np.stack([np.repeat(np.cos(ph), SUBLANES, axis=0),
                   np.repeat(-np.sin(ph), SUBLANES, axis=0)])
    a = np.arange(LANES, dtype=np.float64)
    ps = 2.0 * np.pi * np.outer(a, a) / LANES
    gr, gi = np.cos(ps), -np.sin(ps)
    g_fwd = np.block([[gr, gi], [-gi, gr]])
    g_inv = np.block([[gr, -gi], [gi, gr]])
    as_bf = lambda v: jnp.asarray(v, F32).astype(BF16)
    return as_bf(bd_fwd), as_bf(bd_inv), as_bf(tw), as_bf(g_fwd), as_bf(g_inv)


def _strip_to_pair(s):
    r = s.shape[1] // LANES
    return jnp.concatenate(
        [jnp.concatenate([s[:SUBLANES, n * LANES:(n + 1) * LANES],
                          s[SUBLANES:, n * LANES:(n + 1) * LANES]], axis=1)
         for n in range(r)], axis=0)


def _pair_to_strip(z):
    r = z.shape[0] // SUBLANES
    return jnp.concatenate(
        [jnp.concatenate([z[n * SUBLANES:(n + 1) * SUBLANES, :LANES],
                          z[n * SUBLANES:(n + 1) * SUBLANES, LANES:]], axis=0)
         for n in range(r)], axis=1)


def _fft_fwd(zs, bd_fwd, tr, ti, g_fwd):
    tr_rows = zs[0].shape[0]
    tr2 = jnp.concatenate([tr, tr], axis=1)
    ti2 = jnp.concatenate([ti, ti], axis=1)
    a_s = [jnp.dot(bd_fwd, z.astype(BF16), preferred_element_type=F32) for z in zs]
    b_s = []
    for a in a_s:
        a = a.astype(BF16)
        ar, ai = a[:tr_rows], a[tr_rows:]
        br = ar * tr2 - ai * ti2
        bi = ar * ti2 + ai * tr2
        b_s.append(jnp.concatenate([
            jnp.concatenate([br[:, :LANES], bi[:, :LANES]], axis=1),
            jnp.concatenate([br[:, LANES:], bi[:, LANES:]], axis=1)], axis=0))
    return [jnp.dot(b, g_fwd, preferred_element_type=F32) for b in b_s]


def _cmul_spectrum(x, kr, ki):
    x = x.astype(BF16)
    xr, xi = x[:, :LANES], x[:, LANES:]
    return jnp.concatenate([xr * kr - xi * ki, xr * ki + xi * kr], axis=1)


def _fft_inv(ys, bd_inv, tr, ti, g_inv):
    tr_rows = ys[0].shape[0] // 2
    trs = jnp.concatenate([tr, tr], axis=0)
    tis = jnp.concatenate([ti, ti], axis=0)
    p_s = [jnp.dot(y, g_inv, preferred_element_type=F32) for y in ys]
    q_s = []
    for p in p_s:
        p = p.astype(BF16)
        pr, pi = p[:, :LANES], p[:, LANES:]
        qr = pr * trs + pi * tis
        qi = pi * trs - pr * tis
        q_s.append(jnp.concatenate([
            jnp.concatenate([qr[:tr_rows], qr[tr_rows:]], axis=1),
            jnp.concatenate([qi[:tr_rows], qi[tr_rows:]], axis=1)], axis=0))
    return [jnp.dot(bd_inv, q, preferred_element_type=F32) for q in q_s]


def _filter_mlp_kernel(zT_ref, t_ref, w1_ref, b1_ref, w2_ref, b2_ref, w3_ref, b3_ref,
                       fr_ref, w4_ref, ad_ref, o_ref, *, chunk):
    hp = lax.Precision.HIGHEST
    tl = zT_ref.shape[1]
    fr = fr_ref[...]
    h = jnp.sin(fr * (jnp.dot(w1_ref[...], zT_ref[...], precision=hp,
                              preferred_element_type=F32) + b1_ref[...]))
    h = jnp.sin(fr * (jnp.dot(w2_ref[...], h, precision=hp,
                              preferred_element_type=F32) + b2_ref[...]))
    h = jnp.sin(fr * (jnp.dot(w3_ref[...], h, precision=hp,
                              preferred_element_type=F32) + b3_ref[...]))
    h_hi = h.astype(BF16)
    h_lo = (h - h_hi.astype(F32)).astype(BF16)
    h3 = jnp.concatenate([h_hi, h_lo, h_hi], axis=0)
    t = t_ref[...]
    c = ad_ref.shape[0]
    first = (pl.program_id(0) == 0) & (lax.broadcasted_iota(jnp.int32, (chunk, tl), 1) == 0)
    for cc in range(c // chunk):
        ad = jnp.concatenate([ad_ref[cc * chunk:(cc + 1) * chunk, :]] * (tl // LANES), axis=1)
        decay = jnp.exp(-(t * ad))
        for od in range(2 * HY_ORDER):
            r0 = od * c + cc * chunk
            v = jnp.dot(w4_ref[r0:r0 + chunk, :], h3, preferred_element_type=F32) * decay
            if od % 2 == 1:
                v = jnp.where(first, 0.0, v)
            o_ref[r0:r0 + chunk, :] = v


def _filter_mlp(zT, tvec, w1T, b1, w2T, b2, w3T, b3, fr, w4T, ad, tl):
    nz, seq = zT.shape
    rows, hid = w4T.shape
    full = lambda a: pl.BlockSpec(a.shape, lambda i: (0,) * a.ndim)
    return pl.pallas_call(
        functools.partial(_filter_mlp_kernel, chunk=512),
        grid=(seq // tl,),
        in_specs=[
            pl.BlockSpec((nz, tl), lambda i: (0, i)),
            pl.BlockSpec((1, tl), lambda i: (0, i)),
            full(w1T), full(b1), full(w2T), full(b2), full(w3T), full(b3), full(fr),
            full(w4T), full(ad),
        ],
        out_specs=pl.BlockSpec((rows, tl), lambda i: (0, i)),
        out_shape=jax.ShapeDtypeStruct((rows, seq), F32),
        compiler_params=_cparams(("arbitrary",)),
        name="filter_mlp",
    )(zT, tvec, w1T, b1, w2T, b2, w3T, b3, fr, w4T, ad)


def _filter_fft_kernel(hf_ref, hb_ref, bdf_ref, tw_ref, gf_ref, o_ref, *, strips_per_iter):
    nstrips = hf_ref.shape[0] // STRIP

    def body(it, carry):
        strips = [it * strips_per_iter + k for k in range(strips_per_iter)]
        tiles = [_strip_to_pair(ref[pl.ds(pl.multiple_of(s * STRIP, STRIP), STRIP), :])
                 for s in strips for ref in (hf_ref, hb_ref)]
        xs = _fft_fwd(tiles, bdf_ref[...], tw_ref[0], tw_ref[1], gf_ref[...])
        for k, s in enumerate(strips):
            xf, xb = xs[2 * k], xs[2 * k + 1]
            o_ref[0, s] = jnp.concatenate(
                [xf[:, :LANES] + xb[:, :LANES], xf[:, LANES:] - xb[:, LANES:]],
                axis=1).astype(o_ref.dtype)
        return carry

    lax.fori_loop(0, nstrips // strips_per_iter, body, 0)


def _filter_fft(hT, consts, c, r, cs):
    bd_fwd, _, tw, g_fwd, _ = consts
    seq = hT.shape[1]
    nblk = c // cs
    full = lambda a: pl.BlockSpec(a.shape, lambda o, j: (0,) * a.ndim)
    return pl.pallas_call(
        functools.partial(_filter_fft_kernel, strips_per_iter=2),
        grid=(HY_ORDER, nblk),
        in_specs=[
            pl.BlockSpec((cs, seq), lambda o, j: ((2 * o) * nblk + j, 0)),
            pl.BlockSpec((cs, seq), lambda o, j: ((2 * o + 1) * nblk + j, 0)),
            full(bd_fwd), full(tw), full(g_fwd),
        ],
        out_specs=pl.BlockSpec((1, cs // STRIP, STRIP * r, 2 * LANES),
                               lambda o, j: (o, j, 0, 0)),
        out_shape=jax.ShapeDtypeStruct((HY_ORDER, c // STRIP, STRIP * r, 2 * LANES), BF16),
        compiler_params=_cparams(("arbitrary", "arbitrary")),
        name="filter_fft",
    )(hT, hT, bd_fwd, tw, g_fwd)


def _hyena_kernel(v_ref, x1_ref, x2_ref, par_ref, kh_ref, bdf_ref, bdi_ref, tw_ref, gf_ref,
                  gi_ref, o_ref, *, strips_per_iter):
    seq = v_ref.shape[2]
    r = seq // LANES
    nstrips = v_ref.shape[1] // STRIP
    lane = lax.broadcasted_iota(jnp.int32, (SUBLANES, LANES), 1)

    def short_conv(p, w):
        p32 = pltpu.bitcast(p, jnp.uint32)
        prev = pltpu.roll(p32, 1, axis=1)
        nxt = pltpu.roll(p32, seq - 1, axis=1)
        zero = jnp.uint32(0)
        prev = jnp.concatenate(
            [jnp.where(lane == 0, zero, prev[:, :LANES]), prev[:, LANES:]], axis=1)
        nxt = jnp.concatenate(
            [nxt[:, :seq - LANES], jnp.where(lane == LANES - 1, zero, nxt[:, seq - LANES:])],
            axis=1)
        prev = pltpu.bitcast(prev, BF16)
        nxt = pltpu.bitcast(nxt, BF16)
        wide = lambda k: jnp.concatenate([w[k].astype(BF16)] * r, axis=1)
        u = prev * wide(0) + p * wide(1) + nxt * wide(2) + wide(3)
        return u.astype(F32)

    def body(it, carry):
        strips = [it * strips_per_iter + k for k in range(strips_per_iter)]
        rows = [pl.ds(pl.multiple_of(s * STRIP, STRIP), STRIP) for s in strips]
        tr, ti = tw_ref[0], tw_ref[1]
        part = lambda ref, rw, k0: _strip_to_pair(short_conv(
            ref[0, rw, :], [par_ref[k0 + k, rw, :] for k in range(4)]))
        zs = [part(v_ref, rw, 0) for rw in rows]
        gates = (x1_ref, x2_ref)
        for o in range(HY_ORDER):
            gs = [part(gates[o], rw, 4 + 4 * o) for rw in rows]
            xs = _fft_fwd(zs, bdf_ref[...], tr, ti, gf_ref[...])
            ys = [_cmul_spectrum(x, kh_ref[o, s, :, :LANES], kh_ref[o, s, :, LANES:])
                  for s, x in zip(strips, xs)]
            convs = _fft_inv(ys, bdi_ref[...], tr, ti, gi_ref[...])
            nxt = []
            for rw, z, gate, conv in zip(rows, zs, gs, convs):
                b = par_ref[12 + o, rw, :]
                bias = jnp.concatenate(
                    [jnp.concatenate([b[:SUBLANES], b[SUBLANES:]], axis=1)] * r, axis=0)
                nxt.append(gate * (conv + z * bias))
            zs = nxt
        for rw, z in zip(rows, zs):
            o_ref[0, rw, :] = _pair_to_strip(z).astype(o_ref.dtype)
        return carry

    lax.fori_loop(0, nstrips // strips_per_iter, body, 0)


def _hyena(pT, par, khat, consts, batch, c, cs, strips_per_iter):
    bd_fwd, bd_inv, tw, g_fwd, g_inv = consts
    seq = pT.shape[2]
    nblk = c // cs
    full = lambda a: pl.BlockSpec(a.shape, lambda j, b: (0,) * a.ndim)
    part = lambda k: pl.BlockSpec((1, cs, seq), lambda j, b, k=k: (b, k * nblk + j, 0))
    return pl.pallas_call(
        functools.partial(_hyena_kernel, strips_per_iter=strips_per_iter),
        grid=(nblk, batch),
        in_specs=[
            part(0), part(1), part(2),
            pl.BlockSpec((par.shape[0], cs, LANES), lambda j, b: (0, j, 0)),
            pl.BlockSpec((HY_ORDER, cs // STRIP) + khat.shape[2:], lambda j, b: (0, j, 0, 0)),
            full(bd_fwd), full(bd_inv), full(tw), full(g_fwd), full(g_inv),
        ],
        out_specs=pl.BlockSpec((1, cs, seq), lambda j, b: (b, j, 0)),
        out_shape=jax.ShapeDtypeStruct((batch, c, seq), BF16),
        compiler_params=_cparams(("arbitrary", "arbitrary")),
        name="hyena",
    )(pT, pT, pT, par, khat, bd_fwd, bd_inv, tw, g_fwd, g_inv)


def _nat_kernel(q_ref, k_ref, v_ref, bias_ref, o_ref, *, rows, scale, rows_per_iter):
    npair = q_ref.shape[2] // LANES
    win = NAT_KH * GRID_W
    lo = lax.broadcasted_iota(jnp.int32, (GRID_W, LANES), 1) < LANES // 2

    def body(it, carry):
        chains = []
        for u in range(rows_per_iter):
            i = it * rows_per_iter + u
            rs = jnp.clip(i - NAT_KH // 2, 0, rows - NAT_KH)
            d = rs - i + (NAT_KH - 1)
            q0 = pl.multiple_of(i * GRID_W, GRID_W)
            k0 = pl.multiple_of(rs * GRID_W, GRID_W)
            for p in range(npair):
                chains.append((d, q0, k0, p, slice(p * LANES, (p + 1) * LANES)))
        scores = []
        for d, q0, k0, p, cols in chains:
            qp = q_ref[0, pl.ds(q0, GRID_W), cols].astype(F32) * scale
            q2 = jnp.concatenate([jnp.where(lo, qp, 0.0), jnp.where(lo, 0.0, qp)], axis=0)
            s = lax.dot_general(q2.astype(BF16), k_ref[0, pl.ds(k0, win), cols],
                                (((1,), (1,)), ((), ())), preferred_element_type=F32)
            scores.append(s + bias_ref[d, p])
        probs = []
        for s in scores:
            m = jnp.max(s, axis=-1, keepdims=True)
            pe = jnp.exp((s - m).astype(BF16))
            probs.append((pe, jnp.sum(pe.astype(F32), axis=-1, keepdims=True)))
        for (d, q0, k0, p, cols), (pe, l) in zip(chains, probs):
            o = jnp.dot(pe, v_ref[0, pl.ds(k0, win), cols], preferred_element_type=F32) / l
            o_ref[0, pl.ds(q0, GRID_W), cols] = jnp.where(
                lo, o[:GRID_W], o[GRID_W:]).astype(o_ref.dtype)
        return carry

    lax.fori_loop(0, rows // rows_per_iter, body, 0)


def _nat(pn3, bias, batch, seq, width, hd):
    gw = 2 * LANES
    ng = width // gw
    rows = seq // GRID_W
    blk = lambda k: pl.BlockSpec((1, seq, gw), lambda hg, b, k=k: (b, 0, k * ng + hg))
    return pl.pallas_call(
        functools.partial(_nat_kernel, rows=rows, scale=hd ** -0.5, rows_per_iter=4),
        grid=(ng, batch),
        in_specs=[
            blk(0), blk(1), blk(2),
            pl.BlockSpec((NAT_KH, gw // LANES, 2 * GRID_W, NAT_KH * GRID_W),
                         lambda hg, b: (0, hg, 0, 0)),
        ],
        out_specs=pl.BlockSpec((1, seq, gw), lambda hg, b: (b, 0, hg)),
        out_shape=jax.ShapeDtypeStruct((batch, seq, width), BF16),
        compiler_params=_cparams(("arbitrary", "arbitrary")),
        name="nat",
    )(pn3, pn3, pn3, bias)


def _nat_bias_table(rpb):
    h = rpb.shape[0]
    cols = np.arange(GRID_W)
    col_start = np.clip(cols - NAT_KW // 2, 0, GRID_W - NAT_KW)
    cc = cols[None, :]
    valid = (cc >= col_start[:, None]) & (cc < col_start[:, None] + NAT_KW)
    col_off = cc - cols[:, None] + (NAT_KW - 1)
    pick = (col_off[None] == np.arange(2 * NAT_KW - 1)[:, None, None]) & valid[None]
    pick = jnp.asarray(pick.reshape(2 * NAT_KW - 1, -1), F32)
    nro = rpb.shape[1]
    t = jnp.dot(rpb.astype(F32).reshape(h * nro, -1), pick, precision=lax.Precision.HIGHEST)
    t = t.reshape(h, nro, GRID_W, GRID_W)
    t = jnp.where(valid[None, None], t, MASK_VALUE)
    t = jnp.stack([t[:, d:d + NAT_KH] for d in range(NAT_KH)], axis=0)
    t = jnp.transpose(t, (0, 1, 3, 2, 4))
    return t.reshape(NAT_KH, h // 2, 2 * GRID_W, NAT_KH * GRID_W)


def _out_proj_kernel(x_ref, yh_ref, yn_ref, gh_ref, gn_ref, woh_ref, won_ref, h_ref):
    tm = x_ref.shape[0]
    yh = yh_ref[0].astype(F32)
    ms = jnp.mean(yh * yh, axis=0, keepdims=True)
    gh = jnp.concatenate([gh_ref[...]] * (tm // LANES), axis=1)
    nh = (yh * lax.rsqrt(ms + NORM_EPS) * gh).astype(BF16)
    nn = _rms(yn_ref[...].astype(F32), gn_ref[...]).astype(BF16)
    acc = lax.dot_general(nh, woh_ref[...], (((0,), (0,)), ((), ())),
                          preferred_element_type=F32)
    acc = acc + jnp.dot(nn, won_ref[...], preferred_element_type=F32)
    h_ref[...] = x_ref[...] + acc


def _out_proj(x2, yhT, yn, gh, gn, woh, won, seq, tm):
    ntok, d = x2.shape
    c = yn.shape[1]
    tpb = seq // tm
    full = lambda a: pl.BlockSpec(a.shape, lambda i: (0,) * a.ndim)
    return pl.pallas_call(
        _out_proj_kernel,
        grid=(ntok // tm,),
        in_specs=[
            pl.BlockSpec((tm, d), lambda i: (i, 0)),
            pl.BlockSpec((1, c, tm), lambda i: (i // tpb, 0, i % tpb)),
            pl.BlockSpec((tm, c), lambda i: (i, 0)),
            full(gh), full(gn), full(woh), full(won),
        ],
        out_specs=pl.BlockSpec((tm, d), lambda i: (i, 0)),
        out_shape=jax.ShapeDtypeStruct((ntok, d), F32),
        compiler_params=_cparams(("arbitrary",)),
        name="out_proj",
    )(x2, yhT, yn, gh, gn, woh, won)


def _mlp_kernel(h_ref, g_ref, wu_ref, wd_ref, gf_ref, o_ref, a_ref):
    i, j = pl.program_id(0), pl.program_id(1)
    last = pl.num_programs(1) - 1
    slot = i % 2

    @pl.when((i == 0) & (j == 0))
    def _():
        a_ref[0] = _rms(h_ref[...], g_ref[...]).astype(BF16)

    @pl.when(j == 0)
    def _():
        o_ref[...] = h_ref[...]

    def hidden_tile():
        u = jnp.dot(a_ref[slot], wu_ref[...], preferred_element_type=F32)
        u = jnp.square(jnp.maximum(u, 0.0)).astype(BF16)
        return o_ref[...] + jnp.dot(u, wd_ref[...], preferred_element_type=F32)

    @pl.when(j < last)
    def _():
        o_ref[...] = hidden_tile()

    @pl.when(j == last)
    def _():
        acc = hidden_tile()
        a_ref[1 - slot] = _rms(h_ref[...], g_ref[...]).astype(BF16)
        o_ref[...] = _rms(acc, gf_ref[...])


def _mlp(h, g, wu, wd, gf, tm, tf):
    ntok, d = h.shape
    dff = wu.shape[1]
    nt, nf = ntok // tm, dff // tf
    assert nf >= 2
    h_map = lambda i, j: (jnp.minimum(i + (j >= nf // 2).astype(jnp.int32), nt - 1), 0)
    return pl.pallas_call(
        _mlp_kernel,
        grid=(nt, nf),
        in_specs=[
            pl.BlockSpec((tm, d), h_map),
            pl.BlockSpec((1, d), lambda i, j: (0, 0)),
            pl.BlockSpec((d, tf), lambda i, j: (0, j)),
            pl.BlockSpec((tf, d), lambda i, j: (j, 0)),
            pl.BlockSpec((1, d), lambda i, j: (0, 0)),
        ],
        out_specs=pl.BlockSpec((tm, d), lambda i, j: (i, 0)),
        out_shape=jax.ShapeDtypeStruct((ntok, d), F32),
        scratch_shapes=[pltpu.VMEM((2, tm, d), BF16)],
        compiler_params=_cparams(("arbitrary", "arbitrary")),
        name="mlp",
    )(h, g, wu, wd, gf)


def _positional_features(seq, emb):
    bands = (emb - 1) // 2
    t = np.linspace(0.0, 1.0, seq)
    w_ang = (2.0 * np.pi / seq) * np.arange(seq)
    f = np.linspace(1e-4, bands - 1, bands)
    ang = w_ang[None, :] * f[:, None]
    z = np.concatenate([t[None, :], np.cos(ang), -np.sin(ang)], axis=0)
    pad = (-z.shape[0]) % SUBLANES
    z = np.concatenate([z, np.zeros((pad, seq))], axis=0)
    return jnp.asarray(z, F32), jnp.asarray(t[None, :], F32)


def _lane_rep(v):
    return jnp.broadcast_to(v.astype(F32)[..., None], v.shape + (LANES,))


def _trunk(x, w):
    batch, seq, d = x.shape
    c = w["hy_bias"].shape[-1]
    r = seq // LANES
    rows = seq // GRID_W
    assert seq % LANES == 0 and c % STRIP == 0
    assert rows >= NAT_KH and seq % GRID_W == 0
    consts = _fft_constants(r)
    cs = 64

    emb = w["hy_pe_w1"].shape[0]
    zT, tvec = _positional_features(seq, emb)
    w1T = jnp.pad(w["hy_pe_w1"].astype(F32).T, ((0, 0), (0, zT.shape[0] - emb)))
    col = lambda v: v.astype(F32)[:, None]
    max_decay = math.log(HY_DECAY_TARGET) / HY_FAST_DECAY
    min_decay = math.log(HY_DECAY_TARGET) / HY_SLOW_DECAY
    absd = jnp.asarray(np.abs(np.linspace(min_decay, max_decay, c)), F32)
    w4T = w["hy_pe_w4"].astype(F32).T
    hi_bits = lax.bitcast_convert_type(w4T, jnp.uint32) & jnp.uint32(0xFFFF0000)
    w4_hi_f32 = lax.bitcast_convert_type(hi_bits, F32)
    w4_hi = w4_hi_f32.astype(BF16)
    w4_lo = (w4T - w4_hi_f32).astype(BF16)
    hT = _filter_mlp(zT, tvec, w1T, col(w["hy_pe_b1"]), w["hy_pe_w2"].astype(F32).T,
                     col(w["hy_pe_b2"]), w["hy_pe_w3"].astype(F32).T, col(w["hy_pe_b3"]),
                     col(w["hy_pe_freq"]), jnp.concatenate([w4_hi, w4_hi, w4_lo], axis=1),
                     _lane_rep(absd), tl=512)
    khat = _filter_fft(hT, consts, c, r, cs)

    x2 = x.reshape(batch * seq, d)
    w_in = w["w_in"]
    n_hy = 3 * c
    tc = 1024
    whT = w_in[:, :n_hy].T.astype(BF16).reshape(n_hy // tc, tc, d)
    wn = w_in[:, n_hy:].astype(BF16)
    pT, pn = _in_proj(x2, w["norm_mix_g"].astype(F32)[None, :], whT, wn, batch, seq, tm=1024)

    cw = w["hy_conv_w"].astype(F32).reshape(3, 3, c)
    cb = w["hy_conv_b"].astype(F32).reshape(3, c)
    par = [jnp.concatenate([cw[:, p], cb[p][None]], axis=0) for p in range(3)]
    par = _lane_rep(jnp.concatenate(par + [w["hy_bias"].astype(F32)], axis=0))
    yhT = _hyena(pT, par, khat, consts, batch, c, cs, strips_per_iter=4 if r <= 32 else 2)

    nat_w = pn.shape[1] // 3
    bias = _nat_bias_table(w["nat_rpb"])
    yn = _nat(pn.reshape(batch, seq, 3 * nat_w), bias, batch, seq, nat_w, nat_w // NAT_HEADS)
    yn = yn.reshape(batch * seq, nat_w)

    w_out = w["w_out"].astype(BF16)
    h = _out_proj(x2, yhT, yn, _lane_rep(w["gnorm_hy"]), w["gnorm_nat"].astype(F32)[None, :],
                  w_out[:c], w_out[c:], seq, tm=512)
    y = _mlp(h, w["norm_mlp_g"].astype(F32)[None, :], w["w_up"].astype(BF16),
             w["w_down"].astype(BF16), w["norm_f_g"].astype(F32)[None, :], tm=512, tf=2048)
    return y.reshape(batch, seq, d)


def kernel(x_prompt, x_sample, norm_mix_g, w_in, hy_conv_w, hy_conv_b, hy_pe_w1, hy_pe_b1,
           hy_pe_w2, hy_pe_b2, hy_pe_w3, hy_pe_b3, hy_pe_freq, hy_pe_w4, hy_bias, nat_rpb,
           gnorm_hy, gnorm_nat, w_out, norm_mlp_g, w_up, w_down, norm_f_g):
    assert norm_mix_g.shape[0] == 1, "single-layer trunk"
    w = dict(norm_mix_g=norm_mix_g[0], w_in=w_in[0], hy_conv_w=hy_conv_w[0],
             hy_conv_b=hy_conv_b[0], hy_pe_w1=hy_pe_w1[0], hy_pe_b1=hy_pe_b1[0],
             hy_pe_w2=hy_pe_w2[0], hy_pe_b2=hy_pe_b2[0], hy_pe_w3=hy_pe_w3[0],
             hy_pe_b3=hy_pe_b3[0], hy_pe_freq=hy_pe_freq[0], hy_pe_w4=hy_pe_w4[0],
             hy_bias=hy_bias[0], nat_rpb=nat_rpb[0], gnorm_hy=gnorm_hy[0],
             gnorm_nat=gnorm_nat[0], w_out=w_out[0], norm_mlp_g=norm_mlp_g[0],
             w_up=w_up[0], w_down=w_down[0], norm_f_g=norm_f_g)
    return (_trunk(x_prompt, w), _trunk(x_sample, w))
```

```python
import functools
import math

import numpy as np
import jax
import jax.numpy as jnp
from jax import lax
from jax.experimental import pallas as pl
from jax.experimental.pallas import tpu as pltpu

F32 = jnp.float32
BF16 = jnp.bfloat16

LANES = 128
SUBLANES = 8
STRIP = 2 * SUBLANES
VMEM_LIMIT = 58 * 1024 * 1024

NORM_EPS = 1e-5
GRID_W = 64
NAT_HEADS = 16
NAT_KH = 8
NAT_KW = 16
HY_ORDER = 2
HY_FAST_DECAY = 0.3
HY_SLOW_DECAY = 1.5
HY_DECAY_TARGET = 1e-2
MASK_VALUE = -1e30


def _cparams(sem):
    return pltpu.CompilerParams(dimension_semantics=sem, vmem_limit_bytes=VMEM_LIMIT)


def _rms(x, g):
    ms = jnp.mean(x * x, axis=-1, keepdims=True)
    return x * lax.rsqrt(ms + NORM_EPS) * g


def _in_proj_kernel(x_ref, g_ref, whT_ref, wn_ref, pT_ref, pn_ref, a_ref):
    s = pl.program_id(0)

    def normalise():
        a_ref[s % 2] = _rms(x_ref[...], g_ref[...]).astype(BF16)

    @pl.when(s == 0)
    def _():
        normalise()

    @pl.when(s > 0)
    def _():
        a = a_ref[1 - s % 2]
        pT_ref[0] = lax.dot_general(whT_ref[...], a, (((1,), (1,)), ((), ())),
                                    preferred_element_type=F32).astype(BF16)
        pn_ref[...] = jnp.dot(a, wn_ref[...], preferred_element_type=F32).astype(BF16)
        normalise()


def _in_proj(x2, g, whT, wn, batch, seq, tm):
    ntok, d = x2.shape
    tpb = seq // tm
    nt = ntok // tm
    tile = lambda s: jnp.maximum(s - 1, 0)
    resident = lambda a: pl.BlockSpec(a.shape, lambda s: (0,) * a.ndim,
                                      pipeline_mode=pl.Buffered(1))
    return pl.pallas_call(
        _in_proj_kernel,
        grid=(nt + 1,),
        in_specs=[
            pl.BlockSpec((tm, d), lambda s: (jnp.minimum(s, nt - 1), 0)),
            pl.BlockSpec((1, d), lambda s: (0, 0)),
            resident(whT), resident(wn),
        ],
        out_specs=[
            pl.BlockSpec((1, whT.shape[0], tm), lambda s: (tile(s) // tpb, 0, tile(s) % tpb)),
            pl.BlockSpec((tm, wn.shape[1]), lambda s: (tile(s), 0)),
        ],
        out_shape=[
            jax.ShapeDtypeStruct((batch, whT.shape[0], seq), BF16),
            jax.ShapeDtypeStruct((ntok, wn.shape[1]), BF16),
        ],
        scratch_shapes=[pltpu.VMEM((2, tm, d), BF16)],
        compiler_params=_cparams(("arbitrary",)),
        name="in_proj",
    )(x2, g, whT, wn)


def _fft_constants(r):
    n2f = 2 * r
    m = LANES * n2f
    k2 = np.arange(r, dtype=np.float64)[:, None]
    n2 = np.arange(r, dtype=np.float64)[None, :]
    th = 2.0 * np.pi * n2 * (k2 + 0.5) / n2f
    er, ei = np.cos(th), -np.sin(th)
    eye = np.eye(SUBLANES)
    bd_fwd = np.concatenate([np.kron(er, eye), np.kron(ei, eye)], axis=0)
    bd_inv = np.concatenate([np.kron(er.T, eye), np.kron(ei.T, eye)], axis=1) * (2.0 / m)
    n1 = np.arange(LANES, dtype=np.float64)[None, :]
    ph = 2.0 * np.pi * n1 * (k2 + 0.5) / m
    tw = np.stack([np.repeat(np.cos(ph), SUBLANES, axis=0),
                   np.repeat(-np.sin(ph), SUBLANES, axis=0)])
    a = np.arange(LANES, dtype=np.float64)
    ps = 2.0 * np.pi * np.outer(a, a) / LANES
    gr, gi = np.cos(ps), -np.sin(ps)
    g_fwd = np.block([[gr, gi], [-gi, gr]])
    g_inv = np.block([[gr, -gi], [gi, gr]])
    as_bf = lambda v: jnp.asarray(v, F32).astype(BF16)
    return as_bf(bd_fwd), as_bf(bd_inv), as_bf(tw), as_bf(g_fwd), as_bf(g_inv)


def _strip_to_pair(s):
    r = s.shape[1] // LANES
    return jnp.concatenate(
        [jnp.concatenate([s[:SUBLANES, n * LANES:(n + 1) * LANES],
                          s[SUBLANES:, n * LANES:(n + 1) * LANES]], axis=1)
         for n in range(r)], axis=0)


def _pair_to_strip(z):
    r = z.shape[0] // SUBLANES
    return jnp.concatenate(
        [jnp.concatenate([z[n * SUBLANES:(n + 1) * SUBLANES, :LANES],
                          z[n * SUBLANES:(n + 1) * SUBLANES, LANES:]], axis=0)
         for n in range(r)], axis=1)


def _fft_fwd(zs, bd_fwd, tr, ti, g_fwd):
    tr_rows = zs[0].shape[0]
    tr2 = jnp.concatenate([tr, tr], axis=1)
    ti2 = jnp.concatenate([ti, ti], axis=1)
    a_s = [jnp.dot(bd_fwd, z.astype(BF16), preferred_element_type=F32) for z in zs]
    b_s = []
    for a in a_s:
        a = a.astype(BF16)
        ar, ai = a[:tr_rows], a[tr_rows:]
        br = ar * tr2 - ai * ti2
        bi = ar * ti2 + ai * tr2
        b_s.append(jnp.concatenate([
            jnp.concatenate([br[:, :LANES], bi[:, :LANES]], axis=1),
            jnp.concatenate([br[:, LANES:], bi[:, LANES:]], axis=1)], axis=0))
    return [jnp.dot(b, g_fwd, preferred_element_type=F32) for b in b_s]


def _cmul_spectrum(x, kr, ki):
    x = x.astype(BF16)
    xr, xi = x[:, :LANES], x[:, LANES:]
    return jnp.concatenate([xr * kr - xi * ki, xr * ki + xi * kr], axis=1)


def _fft_inv(ys, bd_inv, tr, ti, g_inv):
    tr_rows = ys[0].shape[0] // 2
    trs = jnp.concatenate([tr, tr], axis=0)
    tis = jnp.concatenate([ti, ti], axis=0)
    p_s = [jnp.dot(y, g_inv, preferred_element_type=F32) for y in ys]
    q_s = []
    for p in p_s:
        p = p.astype(BF16)
        pr, pi = p[:, :LANES], p[:, LANES:]
        qr = pr * trs + pi * tis
        qi = pi * trs - pr * tis
        q_s.append(jnp.concatenate([
            jnp.concatenate([qr[:tr_rows], qr[tr_rows:]], axis=1),
            jnp.concatenate([qi[:tr_rows], qi[tr_rows:]], axis=1)], axis=0))
    return [jnp.dot(bd_inv, q, preferred_element_type=F32) for q in q_s]


def _filter_mlp_kernel(zT_ref, t_ref, w1_ref, b1_ref, w2_ref, b2_ref, w3_ref, b3_ref,
                       fr_ref, w4_ref, ad_ref, o_ref, *, chunk):
    hp = lax.Precision.HIGHEST
    tl = zT_ref.shape[1]
    fr = fr_ref[...]
    h = jnp.sin(fr * (jnp.dot(w1_ref[...], zT_ref[...], precision=hp,
                              preferred_element_type=F32) + b1_ref[...]))
    h = jnp.sin(fr * (jnp.dot(w2_ref[...], h, precision=hp,
                              preferred_element_type=F32) + b2_ref[...]))
    h = jnp.sin(fr * (jnp.dot(w3_ref[...], h, precision=hp,
                              preferred_element_type=F32) + b3_ref[...]))
    h_hi = h.astype(BF16)
    h_lo = (h - h_hi.astype(F32)).astype(BF16)
    h3 = jnp.concatenate([h_hi, h_lo, h_hi], axis=0)
    t = t_ref[...]
    c = ad_ref.shape[0]
    first = (pl.program_id(0) == 0) & (lax.broadcasted_iota(jnp.int32, (chunk, tl), 1) == 0)
    for cc in range(c // chunk):
        ad = jnp.concatenate([ad_ref[cc * chunk:(cc + 1) * chunk, :]] * (tl // LANES), axis=1)
        decay = jnp.exp(-(t * ad))
        for od in range(2 * HY_ORDER):
            r0 = od * c + cc * chunk
            v = jnp.dot(w4_ref[r0:r0 + chunk, :], h3, preferred_element_type=F32) * decay
            if od % 2 == 1:
                v = jnp.where(first, 0.0, v)
            o_ref[r0:r0 + chunk, :] = v


def _filter_mlp(zT, tvec, w1T, b1, w2T, b2, w3T, b3, fr, w4T, ad, tl):
    nz, seq = zT.shape
    rows, hid = w4T.shape
    full = lambda a: pl.BlockSpec(a.shape, lambda i: (0,) * a.ndim)
    return pl.pallas_call(
        functools.partial(_filter_mlp_kernel, chunk=512),
        grid=(seq // tl,),
        in_specs=[
            pl.BlockSpec((nz, tl), lambda i: (0, i)),
            pl.BlockSpec((1, tl), lambda i: (0, i)),
            full(w1T), full(b1), full(w2T), full(b2), full(w3T), full(b3), full(fr),
            full(w4T), full(ad),
        ],
        out_specs=pl.BlockSpec((rows, tl), lambda i: (0, i)),
        out_shape=jax.ShapeDtypeStruct((rows, seq), F32),
        compiler_params=_cparams(("arbitrary",)),
        name="filter_mlp",
    )(zT, tvec, w1T, b1, w2T, b2, w3T, b3, fr, w4T, ad)


def _filter_fft_kernel(hf_ref, hb_ref, bdf_ref, tw_ref, gf_ref, o_ref, *, strips_per_iter):
    nstrips = hf_ref.shape[0] // STRIP

    def body(it, carry):
        strips = [it * strips_per_iter + k for k in range(strips_per_iter)]
        tiles = [_strip_to_pair(ref[pl.ds(pl.multiple_of(s * STRIP, STRIP), STRIP), :])
                 for s in strips for ref in (hf_ref, hb_ref)]
        xs = _fft_fwd(tiles, bdf_ref[...], tw_ref[0], tw_ref[1], gf_ref[...])
        for k, s in enumerate(strips):
            xf, xb = xs[2 * k], xs[2 * k + 1]
            o_ref[0, s] = jnp.concatenate(
                [xf[:, :LANES] + xb[:, :LANES], xf[:, LANES:] - xb[:, LANES:]],
                axis=1).astype(o_ref.dtype)
        return carry

    lax.fori_loop(0, nstrips // strips_per_iter, body, 0)


def _filter_fft(hT, consts, c, r, cs):
    bd_fwd, _, tw, g_fwd, _ = consts
    seq = hT.shape[1]
    nblk = c // cs
    full = lambda a: pl.BlockSpec(a.shape, lambda o, j: (0,) * a.ndim)
    return pl.pallas_call(
        functools.partial(_filter_fft_kernel, strips_per_iter=2),
        grid=(HY_ORDER, nblk),
        in_specs=[
            pl.BlockSpec((cs, seq), lambda o, j: ((2 * o) * nblk + j, 0)),
            pl.BlockSpec((cs, seq), lambda o, j: ((2 * o + 1) * nblk + j, 0)),
            full(bd_fwd), full(tw), full(g_fwd),
        ],
        out_specs=pl.BlockSpec((1, cs // STRIP, STRIP * r, 2 * LANES),
                               lambda o, j: (o, j, 0, 0)),
        out_shape=jax.ShapeDtypeStruct((HY_ORDER, c // STRIP, STRIP * r, 2 * LANES), BF16),
        compiler_params=_cparams(("arbitrary", "arbitrary")),
        name="filter_fft",
    )(hT, hT, bd_fwd, tw, g_fwd)


def _hyena_kernel(v_ref, x1_ref, x2_ref, par_ref, kh_ref, bdf_ref, bdi_ref, tw_ref, gf_ref,
                  gi_ref, o_ref, *, strips_per_iter):
    seq = v_ref.shape[2]
    r = seq // LANES
    nstrips = v_ref.shape[1] // STRIP
    lane = lax.broadcasted_iota(jnp.int32, (SUBLANES, LANES), 1)

    def short_conv(p, w):
        p32 = pltpu.bitcast(p, jnp.uint32)
        prev = pltpu.roll(p32, 1, axis=1)
        nxt = pltpu.roll(p32, seq - 1, axis=1)
        zero = jnp.uint32(0)
        prev = jnp.concatenate(
            [jnp.where(lane == 0, zero, prev[:, :LANES]), prev[:, LANES:]], axis=1)
        nxt = jnp.concatenate(
            [nxt[:, :seq - LANES], jnp.where(lane == LANES - 1, zero, nxt[:, seq - LANES:])],
            axis=1)
        prev = pltpu.bitcast(prev, BF16)
        nxt = pltpu.bitcast(nxt, BF16)
        wide = lambda k: jnp.concatenate([w[k].astype(BF16)] * r, axis=1)
        u = prev * wide(0) + p * wide(1) + nxt * wide(2) + wide(3)
        return u.astype(F32)

    def body(it, carry):
        strips = [it * strips_per_iter + k for k in range(strips_per_iter)]
        rows = [pl.ds(pl.multiple_of(s * STRIP, STRIP), STRIP) for s in strips]
        tr, ti = tw_ref[0], tw_ref[1]
        part = lambda ref, rw, k0: _strip_to_pair(short_conv(
            ref[0, rw, :], [par_ref[k0 + k, rw, :] for k in range(4)]))
        zs = [part(v_ref, rw, 0) for rw in rows]
        gates = (x1_ref, x2_ref)
        for o in range(HY_ORDER):
            gs = [part(gates[o], rw, 4 + 4 * o) for rw in rows]
            xs = _fft_fwd(zs, bdf_ref[...], tr, ti, gf_ref[...])
            ys = [_cmul_spectrum(x, kh_ref[o, s, :, :LANES], kh_ref[o, s, :, LANES:])
                  for s, x in zip(strips, xs)]
            convs = _fft_inv(ys, bdi_ref[...], tr, ti, gi_ref[...])
            nxt = []
            for rw, z, gate, conv in zip(rows, zs, gs, convs):
                b = par_ref[12 + o, rw, :]
                bias = jnp.concatenate(
                    [jnp.concatenate([b[:SUBLANES], b[SUBLANES:]], axis=1)] * r, axis=0)
                nxt.append(gate * (conv + z * bias))
            zs = nxt
        for rw, z in zip(rows, zs):
            o_ref[0, rw, :] = _pair_to_strip(z).astype(o_ref.dtype)
        return carry

    lax.fori_loop(0, nstrips // strips_per_iter, body, 0)


def _hyena(pT, par, khat, consts, batch, c, cs, strips_per_iter):
    bd_fwd, bd_inv, tw, g_fwd, g_inv = consts
    seq = pT.shape[2]
    nblk = c // cs
    full = lambda a: pl.BlockSpec(a.shape, lambda j, b: (0,) * a.ndim)
    part = lambda k: pl.BlockSpec((1, cs, seq), lambda j, b, k=k: (b, k * nblk + j, 0))
    return pl.pallas_call(
        functools.partial(_hyena_kernel, strips_per_iter=strips_per_iter),
        grid=(nblk, batch),
        in_specs=[
            part(0), part(1), part(2),
            pl.BlockSpec((par.shape[0], cs, LANES), lambda j, b: (0, j, 0)),
            pl.BlockSpec((HY_ORDER, cs // STRIP) + khat.shape[2:], lambda j, b: (0, j, 0, 0)),
            full(bd_fwd), full(bd_inv), full(tw), full(g_fwd), full(g_inv),
        ],
        out_specs=pl.BlockSpec((1, cs, seq), lambda j, b: (b, j, 0)),
        out_shape=jax.ShapeDtypeStruct((batch, c, seq), BF16),
        compiler_params=_cparams(("arbitrary", "arbitrary")),
        name="hyena",
    )(pT, pT, pT, par, khat, bd_fwd, bd_inv, tw, g_fwd, g_inv)


def _nat_kernel(q_ref, k_ref, v_ref, bias_ref, o_ref, *, rows, scale, rows_per_iter):
    npair = q_ref.shape[2] // LANES
    win = NAT_KH * GRID_W
    lo = lax.broadcasted_iota(jnp.int32, (GRID_W, LANES), 1) < LANES // 2

    def body(it, carry):
        chains = []
        for u in range(rows_per_iter):
            i = it * rows_per_iter + u
            rs = jnp.clip(i - NAT_KH // 2, 0, rows - NAT_KH)
            d = rs - i + (NAT_KH - 1)
            q0 = pl.multiple_of(i * GRID_W, GRID_W)
            k0 = pl.multiple_of(rs * GRID_W, GRID_W)
            for p in range(npair):
                chains.append((d, q0, k0, p, slice(p * LANES, (p + 1) * LANES)))
        scores = []
        for d, q0, k0, p, cols in chains:
            qp = q_ref[0, pl.ds(q0, GRID_W), cols].astype(F32) * scale
            q2 = jnp.concatenate([jnp.where(lo, qp, 0.0), jnp.where(lo, 0.0, qp)], axis=0)
            s = lax.dot_general(q2.astype(BF16), k_ref[0, pl.ds(k0, win), cols],
                                (((1,), (1,)), ((), ())), preferred_element_type=F32)
            scores.append(s + bias_ref[d, p])
        probs = []
        for s in scores:
            m = jnp.max(s, axis=-1, keepdims=True)
            pe = jnp.exp((s - m).astype(BF16))
            probs.append((pe, jnp.sum(pe.astype(F32), axis=-1, keepdims=True)))
        for (d, q0, k0, p, cols), (pe, l) in zip(chains, probs):
            o = jnp.dot(pe, v_ref[0, pl.ds(k0, win), cols], preferred_element_type=F32) / l
            o_ref[0, pl.ds(q0, GRID_W), cols] = jnp.where(
                lo, o[:GRID_W], o[GRID_W:]).astype(o_ref.dtype)
        return carry

    lax.fori_loop(0, rows // rows_per_iter, body, 0)


def _nat(pn3, bias, batch, seq, width, hd):
    gw = 2 * LANES
    ng = width // gw
    rows = seq // GRID_W
    blk = lambda k: pl.BlockSpec((1, seq, gw), lambda hg, b, k=k: (b, 0, k * ng + hg))
    return pl.pallas_call(
        functools.partial(_nat_kernel, rows=rows, scale=hd ** -0.5, rows_per_iter=4),
        grid=(ng, batch),
        in_specs=[
            blk(0), blk(1), blk(2),
            pl.BlockSpec((NAT_KH, gw // LANES, 2 * GRID_W, NAT_KH * GRID_W),
                         lambda hg, b: (0, hg, 0, 0)),
        ],
        out_specs=pl.BlockSpec((1, seq, gw), lambda hg, b: (b, 0, hg)),
        out_shape=jax.ShapeDtypeStruct((batch, seq, width), BF16),
        compiler_params=_cparams(("arbitrary", "arbitrary")),
        name="nat",
    )(pn3, pn3, pn3, bias)


def _nat_bias_table(rpb):
    h = rpb.shape[0]
    cols = np.arange(GRID_W)
    col_start = np.clip(cols - NAT_KW // 2, 0, GRID_W - NAT_KW)
    cc = cols[None, :]
    valid = (cc >= col_start[:, None]) & (cc < col_start[:, None] + NAT_KW)
    col_off = cc - cols[:, None] + (NAT_KW - 1)
    pick = (col_off[None] == np.arange(2 * NAT_KW - 1)[:, None, None]) & valid[None]
    pick = jnp.asarray(pick.reshape(2 * NAT_KW - 1, -1), F32)
    nro = rpb.shape[1]
    t = jnp.dot(rpb.astype(F32).reshape(h * nro, -1), pick, precision=lax.Precision.HIGHEST)
    t = t.reshape(h, nro, GRID_W, GRID_W)
    t = jnp.where(valid[None, None], t, MASK_VALUE)
    t = jnp.stack([t[:, d:d + NAT_KH] for d in range(NAT_KH)], axis=0)
    t = jnp.transpose(t, (0, 1, 3, 2, 4))
    return t.reshape(NAT_KH, h // 2, 2 * GRID_W, NAT_KH * GRID_W)


def _out_proj_kernel(x_ref, yh_ref, yn_ref, gh_ref, gn_ref, woh_ref, won_ref, h_ref,
                     nh_ref, nn_ref):
    s = pl.program_id(0)
    tm = x_ref.shape[0]

    def normalise():
        yh = yh_ref[0].astype(F32)
        ms = jnp.mean(yh * yh, axis=0, keepdims=True)
        gh = jnp.concatenate([gh_ref[...]] * (tm // LANES), axis=1)
        nh_ref[s % 2] = (yh * lax.rsqrt(ms + NORM_EPS) * gh).astype(BF16)
        nn_ref[s % 2] = _rms(yn_ref[...].astype(F32), gn_ref[...]).astype(BF16)

    @pl.when(s == 0)
    def _():
        normalise()

    @pl.when(s > 0)
    def _():
        acc = lax.dot_general(nh_ref[1 - s % 2], woh_ref[...], (((0,), (0,)), ((), ())),
                              preferred_element_type=F32)
        acc = acc + jnp.dot(nn_ref[1 - s % 2], won_ref[...], preferred_element_type=F32)
        h_ref[...] = x_ref[...] + acc
        normalise()


def _out_proj(x2, yhT, yn, gh, gn, woh, won, seq, tm):
    ntok, d = x2.shape
    c = yn.shape[1]
    tpb = seq // tm
    nt = ntok // tm
    tile = lambda s: jnp.maximum(s - 1, 0)
    nxt = lambda s: jnp.minimum(s, nt - 1)
    full = lambda a: pl.BlockSpec(a.shape, lambda s: (0,) * a.ndim)
    resident = lambda a: pl.BlockSpec(a.shape, lambda s: (0,) * a.ndim,
                                      pipeline_mode=pl.Buffered(1))
    return pl.pallas_call(
        _out_proj_kernel,
        grid=(nt + 1,),
        in_specs=[
            pl.BlockSpec((tm, d), lambda s: (tile(s), 0)),
            pl.BlockSpec((1, c, tm), lambda s: (nxt(s) // tpb, 0, nxt(s) % tpb)),
            pl.BlockSpec((tm, c), lambda s: (nxt(s), 0)),
            full(gh), full(gn), resident(woh), resident(won),
        ],
        out_specs=pl.BlockSpec((tm, d), lambda s: (tile(s), 0)),
        out_shape=jax.ShapeDtypeStruct((ntok, d), F32),
        scratch_shapes=[pltpu.VMEM((2, c, tm), BF16), pltpu.VMEM((2, tm, c), BF16)],
        compiler_params=_cparams(("arbitrary",)),
        name="out_proj",
    )(x2, yhT, yn, gh, gn, woh, won)


def _mlp_kernel(h_ref, g_ref, wu_ref, wd_ref, gf_ref, o_ref, a_ref):
    i, j = pl.program_id(0), pl.program_id(1)
    last = pl.num_programs(1) - 1
    slot = i % 2

    @pl.when((i == 0) & (j == 0))
    def _():
        a_ref[0] = _rms(h_ref[...], g_ref[...]).astype(BF16)

    @pl.when(j == 0)
    def _():
        o_ref[...] = h_ref[...]

    def hidden_tile():
        u = jnp.dot(a_ref[slot], wu_ref[...], preferred_element_type=F32)
        u = jnp.square(jnp.maximum(u, 0.0)).astype(BF16)
        return o_ref[...] + jnp.dot(u, wd_ref[...], preferred_element_type=F32)

    @pl.when(j < last)
    def _():
        o_ref[...] = hidden_tile()

    @pl.when(j == last)
    def _():
        acc = hidden_tile()
        a_ref[1 - slot] = _rms(h_ref[...], g_ref[...]).astype(BF16)
        o_ref[...] = _rms(acc, gf_ref[...])


def _mlp(h, g, wu, wd, gf, tm, tf):
    ntok, d = h.shape
    dff = wu.shape[1]
    nt, nf = ntok // tm, dff // tf
    assert nf >= 2
    h_map = lambda i, j: (jnp.minimum(i + (j >= nf // 2).astype(jnp.int32), nt - 1), 0)
    return pl.pallas_call(
        _mlp_kernel,
        grid=(nt, nf),
        in_specs=[
            pl.BlockSpec((tm, d), h_map),
            pl.BlockSpec((1, d), lambda i, j: (0, 0)),
            pl.BlockSpec((d, tf), lambda i, j: (0, j)),
            pl.BlockSpec((tf, d), lambda i, j: (j, 0)),
            pl.BlockSpec((1, d), lambda i, j: (0, 0)),
        ],
        out_specs=pl.BlockSpec((tm, d), lambda i, j: (i, 0)),
        out_shape=jax.ShapeDtypeStruct((ntok, d), F32),
        scratch_shapes=[pltpu.VMEM((2, tm, d), BF16)],
        compiler_params=_cparams(("arbitrary", "arbitrary")),
        name="mlp",
    )(h, g, wu, wd, gf)


def _positional_features(seq, emb):
    bands = (emb - 1) // 2
    t = np.linspace(0.0, 1.0, seq)
    w_ang = (2.0 * np.pi / seq) * np.arange(seq)
    f = np.linspace(1e-4, bands - 1, bands)
    ang = w_ang[None, :] * f[:, None]
    z = np.concatenate([t[None, :], np.cos(ang), -np.sin(ang)], axis=0)
    pad = (-z.shape[0]) % SUBLANES
    z = np.concatenate([z, np.zeros((pad, seq))], axis=0)
    return jnp.asarray(z, F32), jnp.asarray(t[None, :], F32)


def _lane_rep(v):
    return jnp.broadcast_to(v.astype(F32)[..., None], v.shape + (LANES,))


def _trunk(x, w):
    batch, seq, d = x.shape
    c = w["hy_bias"].shape[-1]
    r = seq // LANES
    rows = seq // GRID_W
    assert seq % LANES == 0 and c % STRIP == 0
    assert rows >= NAT_KH and seq % GRID_W == 0
    consts = _fft_constants(r)
    cs = 64

    emb = w["hy_pe_w1"].shape[0]
    zT, tvec = _positional_features(seq, emb)
    w1T = jnp.pad(w["hy_pe_w1"].astype(F32).T, ((0, 0), (0, zT.shape[0] - emb)))
    col = lambda v: v.astype(F32)[:, None]
    max_decay = math.log(HY_DECAY_TARGET) / HY_FAST_DECAY
    min_decay = math.log(HY_DECAY_TARGET) / HY_SLOW_DECAY
    absd = jnp.asarray(np.abs(np.linspace(min_decay, max_decay, c)), F32)
    w4T = w["hy_pe_w4"].astype(F32).T
    hi_bits = lax.bitcast_convert_type(w4T, jnp.uint32) & jnp.uint32(0xFFFF0000)
    w4_hi_f32 = lax.bitcast_convert_type(hi_bits, F32)
    w4_hi = w4_hi_f32.astype(BF16)
    w4_lo = (w4T - w4_hi_f32).astype(BF16)
    hT = _filter_mlp(zT, tvec, w1T, col(w["hy_pe_b1"]), w["hy_pe_w2"].astype(F32).T,
                     col(w["hy_pe_b2"]), w["hy_pe_w3"].astype(F32).T, col(w["hy_pe_b3"]),
                     col(w["hy_pe_freq"]), jnp.concatenate([w4_hi, w4_hi, w4_lo], axis=1),
                     _lane_rep(absd), tl=512)
    khat = _filter_fft(hT, consts, c, r, cs)

    x2 = x.reshape(batch * seq, d)
    w_in = w["w_in"]
    n_hy = 3 * c
    whT = w_in[:, :n_hy].T.astype(BF16)
    wn = w_in[:, n_hy:].astype(BF16)
    pT, pn = _in_proj(x2, w["norm_mix_g"].astype(F32)[None, :], whT, wn, batch, seq, tm=512)

    cw = w["hy_conv_w"].astype(F32).reshape(3, 3, c)
    cb = w["hy_conv_b"].astype(F32).reshape(3, c)
    par = [jnp.concatenate([cw[:, p], cb[p][None]], axis=0) for p in range(3)]
    par = _lane_rep(jnp.concatenate(par + [w["hy_bias"].astype(F32)], axis=0))
    yhT = _hyena(pT, par, khat, consts, batch, c, cs, strips_per_iter=4 if r <= 32 else 2)

    nat_w = pn.shape[1] // 3
    bias = _nat_bias_table(w["nat_rpb"])
    yn = _nat(pn.reshape(batch, seq, 3 * nat_w), bias, batch, seq, nat_w, nat_w // NAT_HEADS)
    yn = yn.reshape(batch * seq, nat_w)

    w_out = w["w_out"].astype(BF16)
    h = _out_proj(x2, yhT, yn, _lane_rep(w["gnorm_hy"]), w["gnorm_nat"].astype(F32)[None, :],
                  w_out[:c], w_out[c:], seq, tm=512)
    y = _mlp(h, w["norm_mlp_g"].astype(F32)[None, :], w["w_up"].astype(BF16),
             w["w_down"].astype(BF16), w["norm_f_g"].astype(F32)[None, :], tm=512, tf=2048)
    return y.reshape(batch, seq, d)


def kernel(x_prompt, x_sample, norm_mix_g, w_in, hy_conv_w, hy_conv_b, hy_pe_w1, hy_pe_b1,
           hy_pe_w2, hy_pe_b2, hy_pe_w3, hy_pe_b3, hy_pe_freq, hy_pe_w4, hy_bias, nat_rpb,
           gnorm_hy, gnorm_nat, w_out, norm_mlp_g, w_up, w_down, norm_f_g):
    assert norm_mix_g.shape[0] == 1, "single-layer trunk"
    w = dict(norm_mix_g=norm_mix_g[0], w_in=w_in[0], hy_conv_w=hy_conv_w[0],
             hy_conv_b=hy_conv_b[0], hy_pe_w1=hy_pe_w1[0], hy_pe_b1=hy_pe_b1[0],
             hy_pe_w2=hy_pe_w2[0], hy_pe_b2=hy_pe_b2[0], hy_pe_w3=hy_pe_w3[0],
             hy_pe_b3=hy_pe_b3[0], hy_pe_freq=hy_pe_freq[0], hy_pe_w4=hy_pe_w4[0],
             hy_bias=hy_bias[0], nat_rpb=nat_rpb[0], gnorm_hy=gnorm_hy[0],
             gnorm_nat=gnorm_nat[0], w_out=w_out[0], norm_mlp_g=norm_mlp_g[0],
             w_up=w_up[0], w_down=w_down[0], norm_f_g=norm_f_g)
    return (_trunk(x_prompt, w), _trunk(x_sample, w))
```

```python
import functools
import math

import numpy as np
import jax
import jax.numpy as jnp
from jax import lax
from jax.experimental import pallas as pl
from jax.experimental.pallas import tpu as pltpu

F32 = jnp.float32
BF16 = jnp.bfloat16

LANES = 128
SUBLANES = 8
STRIP = 2 * SUBLANES
MXU_DIM = 256
VMEM_LIMIT = 58 * 1024 * 1024

NORM_EPS = 1e-5
GRID_W = 64
NAT_HEADS = 16
NAT_KH = 8
NAT_KW = 16
HY_ORDER = 2
HY_FAST_DECAY = 0.3
HY_SLOW_DECAY = 1.5
HY_DECAY_TARGET = 1e-2
MASK_VALUE = -1e30


def _cparams(sem):
    return pltpu.CompilerParams(dimension_semantics=sem, vmem_limit_bytes=VMEM_LIMIT)


def _rms(x, g):
    ms = jnp.mean(x * x, axis=-1, keepdims=True)
    return x * lax.rsqrt(ms + NORM_EPS) * g


def _in_proj_kernel(x_ref, g_ref, whT_ref, wn_ref, pT_ref, pn_ref, a_ref):
    s = pl.program_id(0)

    def normalise():
        a_ref[s % 2] = _rms(x_ref[...], g_ref[...]).astype(BF16)

    @pl.when(s == 0)
    def _():
        normalise()

    @pl.when(s > 0)
    def _():
        a = a_ref[1 - s % 2]
        pT_ref[0] = lax.dot_general(whT_ref[...], a, (((1,), (1,)), ((), ())),
                                    preferred_element_type=F32).astype(BF16)
        pn_ref[...] = jnp.dot(a, wn_ref[...], preferred_element_type=F32).astype(BF16)
        normalise()


def _in_proj(x2, g, whT, wn, batch, seq, tm):
    ntok, d = x2.shape
    tpb = seq // tm
    nt = ntok // tm
    tile = lambda s: jnp.maximum(s - 1, 0)
    resident = lambda a: pl.BlockSpec(a.shape, lambda s: (0,) * a.ndim,
                                      pipeline_mode=pl.Buffered(1))
    return pl.pallas_call(
        _in_proj_kernel,
        grid=(nt + 1,),
        in_specs=[
            pl.BlockSpec((tm, d), lambda s: (jnp.minimum(s, nt - 1), 0)),
            pl.BlockSpec((1, d), lambda s: (0, 0)),
            resident(whT), resident(wn),
        ],
        out_specs=[
            pl.BlockSpec((1, whT.shape[0], tm), lambda s: (tile(s) // tpb, 0, tile(s) % tpb)),
            pl.BlockSpec((tm, wn.shape[1]), lambda s: (tile(s), 0)),
        ],
        out_shape=[
            jax.ShapeDtypeStruct((batch, whT.shape[0], seq), BF16),
            jax.ShapeDtypeStruct((ntok, wn.shape[1]), BF16),
        ],
        scratch_shapes=[pltpu.VMEM((2, tm, d), BF16)],
        compiler_params=_cparams(("arbitrary",)),
        name="in_proj",
    )(x2, g, whT, wn)


def _tile_channels(r):
    return SUBLANES if SUBLANES * r <= MXU_DIM else SUBLANES // 2


def _fft_constants(r):
    g = _tile_channels(r)
    n2f = 2 * r
    m = LANES * n2f
    k2 = np.arange(r, dtype=np.float64)[:, None]
    n2 = np.arange(r, dtype=np.float64)[None, :]
    th = 2.0 * np.pi * n2 * (k2 + 0.5) / n2f
    er, ei = np.cos(th), -np.sin(th)
    eye = np.eye(g)
    bd_fwd = np.concatenate([np.kron(er, eye), np.kron(ei, eye)], axis=0)
    bd_inv = np.concatenate([np.kron(er.T, eye), np.kron(ei.T, eye)], axis=1) * (2.0 / m)
    n1 = np.arange(LANES, dtype=np.float64)[None, :]
    ph = 2.0 * np.pi * n1 * (k2 + 0.5) / m
    tw = np.stack([np.repeat(np.cos(ph), g, axis=0), np.repeat(-np.sin(ph), g, axis=0)])
    a = np.arange(LANES, dtype=np.float64)
    ps = 2.0 * np.pi * np.outer(a, a) / LANES
    gr, gi = np.cos(ps), -np.sin(ps)
    g_fwd = np.block([[gr, gi], [-gi, gr]])
    g_inv = np.block([[gr, -gi], [gi, gr]])
    as_bf = lambda v: jnp.asarray(v, F32).astype(BF16)
    return as_bf(bd_fwd), as_bf(bd_inv), as_bf(tw), as_bf(g_fwd), as_bf(g_inv)


def _swap_halves(a, b):
    low = lax.broadcasted_iota(jnp.int32, (SUBLANES, LANES), 0) < SUBLANES // 2
    rot = lambda v: pltpu.roll(v, SUBLANES // 2, axis=0)
    return jnp.where(low, a, rot(b)), jnp.where(low, rot(a), b)


def _strip_to_tiles(s):
    r = s.shape[1] // LANES
    blk = lambda h, n: s[h * SUBLANES:(h + 1) * SUBLANES, n * LANES:(n + 1) * LANES]
    if _tile_channels(r) == SUBLANES:
        return [jnp.concatenate(
            [jnp.concatenate([blk(0, n), blk(1, n)], axis=1) for n in range(r)], axis=0)]
    tiles = []
    for h in range(STRIP // SUBLANES):
        rows = [jnp.concatenate(_swap_halves(blk(h, n), blk(h, n + 1)), axis=1)
                for n in range(0, r, 2)]
        tiles.append(jnp.concatenate(rows, axis=0))
    return tiles


def _tiles_to_strip(tiles):
    if len(tiles) == 1:
        z = tiles[0]
        r = z.shape[0] // SUBLANES
        return jnp.concatenate(
            [jnp.concatenate([z[n * SUBLANES:(n + 1) * SUBLANES, :LANES],
                              z[n * SUBLANES:(n + 1) * SUBLANES, LANES:]], axis=0)
             for n in range(r)], axis=1)
    halves = []
    for z in tiles:
        blocks = []
        for k in range(z.shape[0] // SUBLANES):
            v = z[k * SUBLANES:(k + 1) * SUBLANES]
            blocks.extend(_swap_halves(v[:, :LANES], v[:, LANES:]))
        halves.append(jnp.concatenate(blocks, axis=1))
    return jnp.concatenate(halves, axis=0)


def _row_param_tiles(b, r):
    reps = _tile_channels(r) * r // SUBLANES
    if _tile_channels(r) == SUBLANES:
        return [jnp.concatenate([jnp.concatenate([b[:SUBLANES], b[SUBLANES:]], axis=1)] * reps,
                                axis=0)]
    tiles = []
    for h in range(STRIP // SUBLANES):
        v = b[h * SUBLANES:(h + 1) * SUBLANES]
        tiles.append(jnp.concatenate([jnp.concatenate(_swap_halves(v, v), axis=1)] * reps,
                                     axis=0))
    return tiles


def _fft_fwd(zs, bd_fwd, tr, ti, g_fwd):
    tr_rows = zs[0].shape[0]
    tr2 = jnp.concatenate([tr, tr], axis=1)
    ti2 = jnp.concatenate([ti, ti], axis=1)
    a_s = [jnp.dot(bd_fwd, z.astype(BF16), preferred_element_type=F32) for z in zs]
    b_s = []
    for a in a_s:
        a = a.astype(BF16)
        ar, ai = a[:tr_rows], a[tr_rows:]
        br = ar * tr2 - ai * ti2
        bi = ar * ti2 + ai * tr2
        b_s.append(jnp.concatenate([
            jnp.concatenate([br[:, :LANES], bi[:, :LANES]], axis=1),
            jnp.concatenate([br[:, LANES:], bi[:, LANES:]], axis=1)], axis=0))
    return [jnp.dot(b, g_fwd, preferred_element_type=F32) for b in b_s]


def _cmul_spectrum(x, kr, ki):
    x = x.astype(BF16)
    xr, xi = x[:, :LANES], x[:, LANES:]
    return jnp.concatenate([xr * kr - xi * ki, xr * ki + xi * kr], axis=1)


def _fft_inv(ys, bd_inv, tr, ti, g_inv):
    tr_rows = ys[0].shape[0] // 2
    trs = jnp.concatenate([tr, tr], axis=0)
    tis = jnp.concatenate([ti, ti], axis=0)
    p_s = [jnp.dot(y, g_inv, preferred_element_type=F32) for y in ys]
    q_s = []
    for p in p_s:
        p = p.astype(BF16)
        pr, pi = p[:, :LANES], p[:, LANES:]
        qr = pr * trs + pi * tis
        qi = pi * trs - pr * tis
        q_s.append(jnp.concatenate([
            jnp.concatenate([qr[:tr_rows], qr[tr_rows:]], axis=1),
            jnp.concatenate([qi[:tr_rows], qi[tr_rows:]], axis=1)], axis=0))
    return [jnp.dot(bd_inv, q, preferred_element_type=F32) for q in q_s]


def _filter_mlp_kernel(zT_ref, t_ref, w1_ref, b1_ref, w2_ref, b2_ref, w3_ref, b3_ref,
                       fr_ref, w4_ref, ad_ref, o_ref, *, chunk):
    hp = lax.Precision.HIGHEST
    tl = zT_ref.shape[1]
    fr = fr_ref[...]
    h = jnp.sin(fr * (jnp.dot(w1_ref[...], zT_ref[...], precision=hp,
                              preferred_element_type=F32) + b1_ref[...]))
    h = jnp.sin(fr * (jnp.dot(w2_ref[...], h, precision=hp,
                              preferred_element_type=F32) + b2_ref[...]))
    h = jnp.sin(fr * (jnp.dot(w3_ref[...], h, precision=hp,
                              preferred_element_type=F32) + b3_ref[...]))
    h_hi = h.astype(BF16)
    h_lo = (h - h_hi.astype(F32)).astype(BF16)
    h3 = jnp.concatenate([h_hi, h_lo, h_hi], axis=0)
    t = t_ref[...]
    c = ad_ref.shape[0]
    first = (pl.program_id(0) == 0) & (lax.broadcasted_iota(jnp.int32, (chunk, tl), 1) == 0)
    for cc in range(c // chunk):
        ad = jnp.concatenate([ad_ref[cc * chunk:(cc + 1) * chunk, :]] * (tl // LANES), axis=1)
        decay = jnp.exp(-(t * ad))
        for od in range(2 * HY_ORDER):
            r0 = od * c + cc * chunk
            v = jnp.dot(w4_ref[r0:r0 + chunk, :], h3, preferred_element_type=F32) * decay
            if od % 2 == 1:
                v = jnp.where(first, 0.0, v)
            o_ref[r0:r0 + chunk, :] = v


def _filter_mlp(zT, tvec, w1T, b1, w2T, b2, w3T, b3, fr, w4T, ad, tl):
    nz, seq = zT.shape
    rows, hid = w4T.shape
    full = lambda a: pl.BlockSpec(a.shape, lambda i: (0,) * a.ndim)
    return pl.pallas_call(
        functools.partial(_filter_mlp_kernel, chunk=512),
        grid=(seq // tl,),
        in_specs=[
            pl.BlockSpec((nz, tl), lambda i: (0, i)),
            pl.BlockSpec((1, tl), lambda i: (0, i)),
            full(w1T), full(b1), full(w2T), full(b2), full(w3T), full(b3), full(fr),
            full(w4T), full(ad),
        ],
        out_specs=pl.BlockSpec((rows, tl), lambda i: (0, i)),
        out_shape=jax.ShapeDtypeStruct((rows, seq), F32),
        compiler_params=_cparams(("arbitrary",)),
        name="filter_mlp",
    )(zT, tvec, w1T, b1, w2T, b2, w3T, b3, fr, w4T, ad)


def _filter_fft_kernel(hf_ref, hb_ref, bdf_ref, tw_ref, gf_ref, o_ref, *, strips_per_iter):
    nstrips = hf_ref.shape[0] // STRIP

    def body(it, carry):
        strips = [it * strips_per_iter + k for k in range(strips_per_iter)]
        tiles = []
        for s in strips:
            for ref in (hf_ref, hb_ref):
                tiles += _strip_to_tiles(ref[pl.ds(pl.multiple_of(s * STRIP, STRIP), STRIP), :])
        xs = _fft_fwd(tiles, bdf_ref[...], tw_ref[0], tw_ref[1], gf_ref[...])
        per_dir = len(tiles) // (2 * strips_per_iter)
        rows = xs[0].shape[0]
        for k, s in enumerate(strips):
            for t in range(per_dir):
                xf, xb = xs[2 * per_dir * k + t], xs[2 * per_dir * k + per_dir + t]
                o_ref[0, s, t * rows:(t + 1) * rows, :] = jnp.concatenate(
                    [xf[:, :LANES] + xb[:, :LANES], xf[:, LANES:] - xb[:, LANES:]],
                    axis=1).astype(o_ref.dtype)
        return carry

    lax.fori_loop(0, nstrips // strips_per_iter, body, 0)


def _filter_fft(hT, consts, c, r, cs):
    bd_fwd, _, tw, g_fwd, _ = consts
    seq = hT.shape[1]
    nblk = c // cs
    full = lambda a: pl.BlockSpec(a.shape, lambda o, j: (0,) * a.ndim)
    return pl.pallas_call(
        functools.partial(_filter_fft_kernel, strips_per_iter=2),
        grid=(HY_ORDER, nblk),
        in_specs=[
            pl.BlockSpec((cs, seq), lambda o, j: ((2 * o) * nblk + j, 0)),
            pl.BlockSpec((cs, seq), lambda o, j: ((2 * o + 1) * nblk + j, 0)),
            full(bd_fwd), full(tw), full(g_fwd),
        ],
        out_specs=pl.BlockSpec((1, cs // STRIP, STRIP * r, 2 * LANES),
                               lambda o, j: (o, j, 0, 0)),
        out_shape=jax.ShapeDtypeStruct((HY_ORDER, c // STRIP, STRIP * r, 2 * LANES), BF16),
        compiler_params=_cparams(("arbitrary", "arbitrary")),
        name="filter_fft",
    )(hT, hT, bd_fwd, tw, g_fwd)


def _hyena_kernel(v_ref, x1_ref, x2_ref, par_ref, kh_ref, bdf_ref, bdi_ref, tw_ref, gf_ref,
                  gi_ref, o_ref, *, strips_per_iter):
    seq = v_ref.shape[2]
    r = seq // LANES
    nstrips = v_ref.shape[1] // STRIP
    lane = lax.broadcasted_iota(jnp.int32, (SUBLANES, LANES), 1)

    def short_conv(p, w):
        p32 = pltpu.bitcast(p, jnp.uint32)
        prev = pltpu.roll(p32, 1, axis=1)
        nxt = pltpu.roll(p32, seq - 1, axis=1)
        zero = jnp.uint32(0)
        prev = jnp.concatenate(
            [jnp.where(lane == 0, zero, prev[:, :LANES]), prev[:, LANES:]], axis=1)
        nxt = jnp.concatenate(
            [nxt[:, :seq - LANES], jnp.where(lane == LANES - 1, zero, nxt[:, seq - LANES:])],
            axis=1)
        prev = pltpu.bitcast(prev, BF16)
        nxt = pltpu.bitcast(nxt, BF16)
        wide = lambda k: jnp.concatenate([w[k].astype(BF16)] * r, axis=1)
        u = prev * wide(0) + p * wide(1) + nxt * wide(2) + wide(3)
        return u.astype(F32)

    def body(it, carry):
        strips = [it * strips_per_iter + k for k in range(strips_per_iter)]
        rows = [pl.ds(pl.multiple_of(s * STRIP, STRIP), STRIP) for s in strips]
        tr, ti = tw_ref[0], tw_ref[1]
        def part(ref, k0):
            tiles = []
            for rw in rows:
                tiles += _strip_to_tiles(short_conv(
                    ref[0, rw, :], [par_ref[k0 + k, rw, :] for k in range(4)]))
            return tiles

        zs = part(v_ref, 0)
        per_strip = len(zs) // strips_per_iter
        spec_rows = 2 * zs[0].shape[0]
        gates = (x1_ref, x2_ref)
        for o in range(HY_ORDER):
            gs = part(gates[o], 4 + 4 * o)
            xs = _fft_fwd(zs, bdf_ref[...], tr, ti, gf_ref[...])
            ys = []
            for n, x in enumerate(xs):
                s, t = strips[n // per_strip], n % per_strip
                kh = kh_ref.at[o, s, t * spec_rows:(t + 1) * spec_rows]
                ys.append(_cmul_spectrum(x, kh[:, :LANES], kh[:, LANES:]))
            convs = _fft_inv(ys, bdi_ref[...], tr, ti, gi_ref[...])
            biases = []
            for rw in rows:
                biases += _row_param_tiles(par_ref[12 + o, rw, :], r)
            zs = [gate * (conv + z * bias)
                  for z, gate, conv, bias in zip(zs, gs, convs, biases)]
        for k, rw in enumerate(rows):
            o_ref[0, rw, :] = _tiles_to_strip(
                zs[k * per_strip:(k + 1) * per_strip]).astype(o_ref.dtype)
        return carry

    lax.fori_loop(0, nstrips // strips_per_iter, body, 0)


def _hyena(pT, par, khat, consts, batch, c, cs, strips_per_iter):
    bd_fwd, bd_inv, tw, g_fwd, g_inv = consts
    seq = pT.shape[2]
    nblk = c // cs
    full = lambda a: pl.BlockSpec(a.shape, lambda j, b: (0,) * a.ndim)
    part = lambda k: pl.BlockSpec((1, cs, seq), lambda j, b, k=k: (b, k * nblk + j, 0))
    return pl.pallas_call(
        functools.partial(_hyena_kernel, strips_per_iter=strips_per_iter),
        grid=(nblk, batch),
        in_specs=[
            part(0), part(1), part(2),
            pl.BlockSpec((par.shape[0], cs, LANES), lambda j, b: (0, j, 0)),
            pl.BlockSpec((HY_ORDER, cs // STRIP) + khat.shape[2:], lambda j, b: (0, j, 0, 0)),
            full(bd_fwd), full(bd_inv), full(tw), full(g_fwd), full(g_inv),
        ],
        out_specs=pl.BlockSpec((1, cs, seq), lambda j, b: (b, j, 0)),
        out_shape=jax.ShapeDtypeStruct((batch, c, seq), BF16),
        compiler_params=_cparams(("arbitrary", "arbitrary")),
        name="hyena",
    )(pT, pT, pT, par, khat, bd_fwd, bd_inv, tw, g_fwd, g_inv)


def _nat_kernel(q_ref, k_ref, v_ref, bias_ref, o_ref, *, rows, scale, rows_per_iter):
    npair = q_ref.shape[2] // LANES
    win = NAT_KH * GRID_W
    lo = lax.broadcasted_iota(jnp.int32, (GRID_W, LANES), 1) < LANES // 2

    def body(it, carry):
        chains = []
        for u in range(rows_per_iter):
            i = it * rows_per_iter + u
            rs = jnp.clip(i - NAT_KH // 2, 0, rows - NAT_KH)
            d = rs - i + (NAT_KH - 1)
            q0 = pl.multiple_of(i * GRID_W, GRID_W)
            k0 = pl.multiple_of(rs * GRID_W, GRID_W)
            for p in range(npair):
                chains.append((d, q0, k0, p, slice(p * LANES, (p + 1) * LANES)))
        scores = []
        for d, q0, k0, p, cols in chains:
            qp = q_ref[0, pl.ds(q0, GRID_W), cols].astype(F32) * scale
            q2 = jnp.concatenate([jnp.where(lo, qp, 0.0), jnp.where(lo, 0.0, qp)], axis=0)
            s = lax.dot_general(q2.astype(BF16), k_ref[0, pl.ds(k0, win), cols],
                                (((1,), (1,)), ((), ())), preferred_element_type=F32)
            scores.append(s + bias_ref[d, p])
        probs = []
        for s in scores:
            m = jnp.max(s, axis=-1, keepdims=True)
            pe = jnp.exp((s - m).astype(BF16))
            probs.append((pe, jnp.sum(pe.astype(F32), axis=-1, keepdims=True)))
        for (d, q0, k0, p, cols), (pe, l) in zip(chains, probs):
            o = jnp.dot(pe, v_ref[0, pl.ds(k0, win), cols], preferred_element_type=F32) / l
            o_ref[0, pl.ds(q0, GRID_W), cols] = jnp.where(
                lo, o[:GRID_W], o[GRID_W:]).astype(o_ref.dtype)
        return carry

    lax.fori_loop(0, rows // rows_per_iter, body, 0)


def _nat(pn3, bias, batch, seq, width, hd):
    gw = 2 * LANES
    ng = width // gw
    rows = seq // GRID_W
    blk = lambda k: pl.BlockSpec((1, seq, gw), lambda hg, b, k=k: (b, 0, k * ng + hg))
    return pl.pallas_call(
        functools.partial(_nat_kernel, rows=rows, scale=hd ** -0.5, rows_per_iter=4),
        grid=(ng, batch),
        in_specs=[
            blk(0), blk(1), blk(2),
            pl.BlockSpec((NAT_KH, gw // LANES, 2 * GRID_W, NAT_KH * GRID_W),
                         lambda hg, b: (0, hg, 0, 0)),
        ],
        out_specs=pl.BlockSpec((1, seq, gw), lambda hg, b: (b, 0, hg)),
        out_shape=jax.ShapeDtypeStruct((batch, seq, width), BF16),
        compiler_params=_cparams(("arbitrary", "arbitrary")),
        name="nat",
    )(pn3, pn3, pn3, bias)


def _nat_bias_table(rpb):
    h = rpb.shape[0]
    cols = np.arange(GRID_W)
    col_start = np.clip(cols - NAT_KW // 2, 0, GRID_W - NAT_KW)
    cc = cols[None, :]
    valid = (cc >= col_start[:, None]) & (cc < col_start[:, None] + NAT_KW)
    col_off = cc - cols[:, None] + (NAT_KW - 1)
    pick = (col_off[None] == np.arange(2 * NAT_KW - 1)[:, None, None]) & valid[None]
    pick = jnp.asarray(pick.reshape(2 * NAT_KW - 1, -1), F32)
    nro = rpb.shape[1]
    t = jnp.dot(rpb.astype(F32).reshape(h * nro, -1), pick, precision=lax.Precision.HIGHEST)
    t = t.reshape(h, nro, GRID_W, GRID_W)
    t = jnp.where(valid[None, None], t, MASK_VALUE)
    t = jnp.stack([t[:, d:d + NAT_KH] for d in range(NAT_KH)], axis=0)
    t = jnp.transpose(t, (0, 1, 3, 2, 4))
    return t.reshape(NAT_KH, h // 2, 2 * GRID_W, NAT_KH * GRID_W)


def _out_proj_kernel(x_ref, yh_ref, yn_ref, gh_ref, gn_ref, woh_ref, won_ref, h_ref,
                     nh_ref, nn_ref):
    s = pl.program_id(0)
    tm = x_ref.shape[0]

    def normalise():
        yh = yh_ref[0].astype(F32)
        ms = jnp.mean(yh * yh, axis=0, keepdims=True)
        gh = jnp.concatenate([gh_ref[...]] * (tm // LANES), axis=1)
        nh_ref[s % 2] = (yh * lax.rsqrt(ms + NORM_EPS) * gh).astype(BF16)
        nn_ref[s % 2] = _rms(yn_ref[...].astype(F32), gn_ref[...]).astype(BF16)

    @pl.when(s == 0)
    def _():
        normalise()

    @pl.when(s > 0)
    def _():
        acc = lax.dot_general(nh_ref[1 - s % 2], woh_ref[...], (((0,), (0,)), ((), ())),
                              preferred_element_type=F32)
        acc = acc + jnp.dot(nn_ref[1 - s % 2], won_ref[...], preferred_element_type=F32)
        h_ref[...] = x_ref[...] + acc
        normalise()


def _out_proj(x2, yhT, yn, gh, gn, woh, won, seq, tm):
    ntok, d = x2.shape
    c = yn.shape[1]
    tpb = seq // tm
    nt = ntok // tm
    tile = lambda s: jnp.maximum(s - 1, 0)
    nxt = lambda s: jnp.minimum(s, nt - 1)
    full = lambda a: pl.BlockSpec(a.shape, lambda s: (0,) * a.ndim)
    resident = lambda a: pl.BlockSpec(a.shape, lambda s: (0,) * a.ndim,
                                      pipeline_mode=pl.Buffered(1))
    return pl.pallas_call(
        _out_proj_kernel,
        grid=(nt + 1,),
        in_specs=[
            pl.BlockSpec((tm, d), lambda s: (tile(s), 0)),
            pl.BlockSpec((1, c, tm), lambda s: (nxt(s) // tpb, 0, nxt(s) % tpb)),
            pl.BlockSpec((tm, c), lambda s: (nxt(s), 0)),
            full(gh), full(gn), resident(woh), resident(won),
        ],
        out_specs=pl.BlockSpec((tm, d), lambda s: (tile(s), 0)),
        out_shape=jax.ShapeDtypeStruct((ntok, d), F32),
        scratch_shapes=[pltpu.VMEM((2, c, tm), BF16), pltpu.VMEM((2, tm, c), BF16)],
        compiler_params=_cparams(("arbitrary",)),
        name="out_proj",
    )(x2, yhT, yn, gh, gn, woh, won)


def _mlp_kernel(h_ref, g_ref, wu_ref, wd_ref, gf_ref, o_ref, a_ref):
    i, j = pl.program_id(0), pl.program_id(1)
    last = pl.num_programs(1) - 1
    slot = i % 2

    @pl.when((i == 0) & (j == 0))
    def _():
        a_ref[0] = _rms(h_ref[...], g_ref[...]).astype(BF16)

    @pl.when(j == 0)
    def _():
        o_ref[...] = h_ref[...]

    def hidden_tile():
        u = jnp.dot(a_ref[slot], wu_ref[...], preferred_element_type=F32)
        u = jnp.square(jnp.maximum(u, 0.0)).astype(BF16)
        return o_ref[...] + jnp.dot(u, wd_ref[...], preferred_element_type=F32)

    @pl.when(j < last)
    def _():
        o_ref[...] = hidden_tile()

    @pl.when(j == last)
    def _():
        acc = hidden_tile()
        a_ref[1 - slot] = _rms(h_ref[...], g_ref[...]).astype(BF16)
        o_ref[...] = _rms(acc, gf_ref[...])


def _mlp(h, g, wu, wd, gf, tm, tf):
    ntok, d = h.shape
    dff = wu.shape[1]
    nt, nf = ntok // tm, dff // tf
    assert nf >= 2
    h_map = lambda i, j: (jnp.minimum(i + (j >= nf // 2).astype(jnp.int32), nt - 1), 0)
    return pl.pallas_call(
        _mlp_kernel,
        grid=(nt, nf),
        in_specs=[
            pl.BlockSpec((tm, d), h_map),
            pl.BlockSpec((1, d), lambda i, j: (0, 0)),
            pl.BlockSpec((d, tf), lambda i, j: (0, j)),
            pl.BlockSpec((tf, d), lambda i, j: (j, 0)),
            pl.BlockSpec((1, d), lambda i, j: (0, 0)),
        ],
        out_specs=pl.BlockSpec((tm, d), lambda i, j: (i, 0)),
        out_shape=jax.ShapeDtypeStruct((ntok, d), F32),
        scratch_shapes=[pltpu.VMEM((2, tm, d), BF16)],
        compiler_params=_cparams(("arbitrary", "arbitrary")),
        name="mlp",
    )(h, g, wu, wd, gf)


def _positional_features(seq, emb):
    bands = (emb - 1) // 2
    t = np.linspace(0.0, 1.0, seq)
    w_ang = (2.0 * np.pi / seq) * np.arange(seq)
    f = np.linspace(1e-4, bands - 1, bands)
    ang = w_ang[None, :] * f[:, None]
    z = np.concatenate([t[None, :], np.cos(ang), -np.sin(ang)], axis=0)
    pad = (-z.shape[0]) % SUBLANES
    z = np.concatenate([z, np.zeros((pad, seq))], axis=0)
    return jnp.asarray(z, F32), jnp.asarray(t[None, :], F32)


def _lane_rep(v):
    return jnp.broadcast_to(v.astype(F32)[..., None], v.shape + (LANES,))


def _trunk(x, w):
    batch, seq, d = x.shape
    c = w["hy_bias"].shape[-1]
    r = seq // LANES
    rows = seq // GRID_W
    assert seq % LANES == 0 and c % STRIP == 0
    assert rows >= NAT_KH and seq % GRID_W == 0
    consts = _fft_constants(r)
    cs = 128

    emb = w["hy_pe_w1"].shape[0]
    zT, tvec = _positional_features(seq, emb)
    w1T = jnp.pad(w["hy_pe_w1"].astype(F32).T, ((0, 0), (0, zT.shape[0] - emb)))
    col = lambda v: v.astype(F32)[:, None]
    max_decay = math.log(HY_DECAY_TARGET) / HY_FAST_DECAY
    min_decay = math.log(HY_DECAY_TARGET) / HY_SLOW_DECAY
    absd = jnp.asarray(np.abs(np.linspace(min_decay, max_decay, c)), F32)
    w4T = w["hy_pe_w4"].astype(F32).T
    hi_bits = lax.bitcast_convert_type(w4T, jnp.uint32) & jnp.uint32(0xFFFF0000)
    w4_hi_f32 = lax.bitcast_convert_type(hi_bits, F32)
    w4_hi = w4_hi_f32.astype(BF16)
    w4_lo = (w4T - w4_hi_f32).astype(BF16)
    hT = _filter_mlp(zT, tvec, w1T, col(w["hy_pe_b1"]), w["hy_pe_w2"].astype(F32).T,
                     col(w["hy_pe_b2"]), w["hy_pe_w3"].astype(F32).T, col(w["hy_pe_b3"]),
                     col(w["hy_pe_freq"]), jnp.concatenate([w4_hi, w4_hi, w4_lo], axis=1),
                     _lane_rep(absd), tl=512)
    khat = _filter_fft(hT, consts, c, r, cs)

    x2 = x.reshape(batch * seq, d)
    w_in = w["w_in"]
    n_hy = 3 * c
    whT = w_in[:, :n_hy].T.astype(BF16)
    wn = w_in[:, n_hy:].astype(BF16)
    pT, pn = _in_proj(x2, w["norm_mix_g"].astype(F32)[None, :], whT, wn, batch, seq, tm=512)

    cw = w["hy_conv_w"].astype(F32).reshape(3, 3, c)
    cb = w["hy_conv_b"].astype(F32).reshape(3, c)
    par = [jnp.concatenate([cw[:, p], cb[p][None]], axis=0) for p in range(3)]
    par = _lane_rep(jnp.concatenate(par + [w["hy_bias"].astype(F32)], axis=0))
    yhT = _hyena(pT, par, khat, consts, batch, c, cs, strips_per_iter=4 if r <= 32 else 2)

    nat_w = pn.shape[1] // 3
    bias = _nat_bias_table(w["nat_rpb"])
    yn = _nat(pn.reshape(batch, seq, 3 * nat_w), bias, batch, seq, nat_w, nat_w // NAT_HEADS)
    yn = yn.reshape(batch * seq, nat_w)

    w_out = w["w_out"].astype(BF16)
    h = _out_proj(x2, yhT, yn, _lane_rep(w["gnorm_hy"]), w["gnorm_nat"].astype(F32)[None, :],
                  w_out[:c], w_out[c:], seq, tm=512)
    y = _mlp(h, w["norm_mlp_g"].astype(F32)[None, :], w["w_up"].astype(BF16),
             w["w_down"].astype(BF16), w["norm_f_g"].astype(F32)[None, :], tm=512, tf=2048)
    return y.reshape(batch, seq, d)


def kernel(x_prompt, x_sample, norm_mix_g, w_in, hy_conv_w, hy_conv_b, hy_pe_w1, hy_pe_b1,
           hy_pe_w2, hy_pe_b2, hy_pe_w3, hy_pe_b3, hy_pe_freq, hy_pe_w4, hy_bias, nat_rpb,
           gnorm_hy, gnorm_nat, w_out, norm_mlp_g, w_up, w_down, norm_f_g):
    assert norm_mix_g.shape[0] == 1, "single-layer trunk"
    w = dict(norm_mix_g=norm_mix_g[0], w_in=w_in[0], hy_conv_w=hy_conv_w[0],
             hy_conv_b=hy_conv_b[0], hy_pe_w1=hy_pe_w1[0], hy_pe_b1=hy_pe_b1[0],
             hy_pe_w2=hy_pe_w2[0], hy_pe_b2=hy_pe_b2[0], hy_pe_w3=hy_pe_w3[0],
             hy_pe_b3=hy_pe_b3[0], hy_pe_freq=hy_pe_freq[0], hy_pe_w4=hy_pe_w4[0],
             hy_bias=hy_bias[0], nat_rpb=nat_rpb[0], gnorm_hy=gnorm_hy[0],
             gnorm_nat=gnorm_nat[0], w_out=w_out[0], norm_mlp_g=norm_mlp_g[0],
             w_up=w_up[0], w_down=w_down[0], norm_f_g=norm_f_g)
    return (_trunk(x_prompt, w), _trunk(x_sample, w))
```

```python
import functools
import math

import numpy as np
import jax
import jax.numpy as jnp
from jax import lax
from jax.experimental import pallas as pl
from jax.experimental.pallas import tpu as pltpu

F32 = jnp.float32
BF16 = jnp.bfloat16

LANES = 128
SUBLANES = 8
STRIP = 2 * SUBLANES
MXU_DIM = 256
VMEM_LIMIT = 58 * 1024 * 1024

PROJ_TM = 512
MLP_TF = 2048
FILTER_TL = 512
FILTER_CHUNK = 512
HYENA_CS = 128
FILTER_STRIPS_PER_ITER = 2
NAT_ROWS_PER_ITER = 4

NORM_EPS = 1e-5
GRID_W = 64
NAT_HEADS = 16
NAT_KH = 8
NAT_KW = 16
HY_ORDER = 2
HY_FAST_DECAY = 0.3
HY_SLOW_DECAY = 1.5
HY_DECAY_TARGET = 1e-2
MASK_VALUE = -1e30


def _cparams(sem):
    return pltpu.CompilerParams(dimension_semantics=sem, vmem_limit_bytes=VMEM_LIMIT)


def _rms(x, g):
    ms = jnp.mean(x * x, axis=-1, keepdims=True)
    return x * lax.rsqrt(ms + NORM_EPS) * g


def _in_proj_kernel(x_ref, g_ref, whT_ref, wn_ref, pT_ref, pn_ref, a_ref):
    s = pl.program_id(0)

    def normalise():
        a_ref[s % 2] = _rms(x_ref[...], g_ref[...]).astype(BF16)

    @pl.when(s == 0)
    def _():
        normalise()

    @pl.when(s > 0)
    def _():
        a = a_ref[1 - s % 2]
        pT_ref[0] = lax.dot_general(whT_ref[...], a, (((1,), (1,)), ((), ())),
                                    preferred_element_type=F32).astype(BF16)
        pn_ref[...] = jnp.dot(a, wn_ref[...], preferred_element_type=F32).astype(BF16)
        normalise()


def _in_proj(x2, g, whT, wn, batch, seq, tm):
    ntok, d = x2.shape
    tpb = seq // tm
    nt = ntok // tm
    tile = lambda s: jnp.maximum(s - 1, 0)
    resident = lambda a: pl.BlockSpec(a.shape, lambda s: (0,) * a.ndim,
                                      pipeline_mode=pl.Buffered(1))
    return pl.pallas_call(
        _in_proj_kernel,
        grid=(nt + 1,),
        in_specs=[
            pl.BlockSpec((tm, d), lambda s: (jnp.minimum(s, nt - 1), 0)),
            pl.BlockSpec((1, d), lambda s: (0, 0)),
            resident(whT), resident(wn),
        ],
        out_specs=[
            pl.BlockSpec((1, whT.shape[0], tm), lambda s: (tile(s) // tpb, 0, tile(s) % tpb)),
            pl.BlockSpec((tm, wn.shape[1]), lambda s: (tile(s), 0)),
        ],
        out_shape=[
            jax.ShapeDtypeStruct((batch, whT.shape[0], seq), BF16),
            jax.ShapeDtypeStruct((ntok, wn.shape[1]), BF16),
        ],
        scratch_shapes=[pltpu.VMEM((2, tm, d), BF16)],
        compiler_params=_cparams(("arbitrary",)),
        name="in_proj",
    )(x2, g, whT, wn)


def _tile_channels(r):
    return SUBLANES if SUBLANES * r <= MXU_DIM else SUBLANES // 2


def _fft_constants(r):
    g = _tile_channels(r)
    n2f = 2 * r
    m = LANES * n2f
    k2 = np.arange(r, dtype=np.float64)[:, None]
    n2 = np.arange(r, dtype=np.float64)[None, :]
    th = 2.0 * np.pi * n2 * (k2 + 0.5) / n2f
    er, ei = np.cos(th), -np.sin(th)
    eye = np.eye(g)
    bd_fwd = np.concatenate([np.kron(er, eye), np.kron(ei, eye)], axis=0)
    bd_inv = np.concatenate([np.kron(er.T, eye), np.kron(ei.T, eye)], axis=1) * (2.0 / m)
    n1 = np.arange(LANES, dtype=np.float64)[None, :]
    ph = 2.0 * np.pi * n1 * (k2 + 0.5) / m
    tw = np.stack([np.repeat(np.cos(ph), g, axis=0), np.repeat(-np.sin(ph), g, axis=0)])
    a = np.arange(LANES, dtype=np.float64)
    ps = 2.0 * np.pi * np.outer(a, a) / LANES
    gr, gi = np.cos(ps), -np.sin(ps)
    g_fwd = np.block([[gr, gi], [-gi, gr]])
    g_inv = np.block([[gr, -gi], [gi, gr]])
    as_bf = lambda v: jnp.asarray(v, F32).astype(BF16)
    return as_bf(bd_fwd), as_bf(bd_inv), as_bf(tw), as_bf(g_fwd), as_bf(g_inv)


def _swap_halves(a, b):
    low = lax.broadcasted_iota(jnp.int32, (SUBLANES, LANES), 0) < SUBLANES // 2
    rot = lambda v: pltpu.roll(v, SUBLANES // 2, axis=0)
    return jnp.where(low, a, rot(b)), jnp.where(low, rot(a), b)


def _strip_to_tiles(s):
    r = s.shape[1] // LANES
    blk = lambda h, n: s[h * SUBLANES:(h + 1) * SUBLANES, n * LANES:(n + 1) * LANES]
    if _tile_channels(r) == SUBLANES:
        return [jnp.concatenate(
            [jnp.concatenate([blk(0, n), blk(1, n)], axis=1) for n in range(r)], axis=0)]
    tiles = []
    for h in range(STRIP // SUBLANES):
        rows = [jnp.concatenate(_swap_halves(blk(h, n), blk(h, n + 1)), axis=1)
                for n in range(0, r, 2)]
        tiles.append(jnp.concatenate(rows, axis=0))
    return tiles


def _tiles_to_strip(tiles):
    if len(tiles) == 1:
        z = tiles[0]
        r = z.shape[0] // SUBLANES
        return jnp.concatenate(
            [jnp.concatenate([z[n * SUBLANES:(n + 1) * SUBLANES, :LANES],
                              z[n * SUBLANES:(n + 1) * SUBLANES, LANES:]], axis=0)
             for n in range(r)], axis=1)
    halves = []
    for z in tiles:
        blocks = []
        for k in range(z.shape[0] // SUBLANES):
            v = z[k * SUBLANES:(k + 1) * SUBLANES]
            blocks.extend(_swap_halves(v[:, :LANES], v[:, LANES:]))
        halves.append(jnp.concatenate(blocks, axis=1))
    return jnp.concatenate(halves, axis=0)


def _row_param_tiles(b, r):
    reps = _tile_channels(r) * r // SUBLANES
    if _tile_channels(r) == SUBLANES:
        return [jnp.concatenate([jnp.concatenate([b[:SUBLANES], b[SUBLANES:]], axis=1)] * reps,
                                axis=0)]
    tiles = []
    for h in range(STRIP // SUBLANES):
        v = b[h * SUBLANES:(h + 1) * SUBLANES]
        tiles.append(jnp.concatenate([jnp.concatenate(_swap_halves(v, v), axis=1)] * reps,
                                     axis=0))
    return tiles


def _fft_fwd(zs, bd_fwd, tr, ti, g_fwd):
    tr_rows = zs[0].shape[0]
    tr2 = jnp.concatenate([tr, tr], axis=1)
    ti2 = jnp.concatenate([ti, ti], axis=1)
    a_s = [jnp.dot(bd_fwd, z.astype(BF16), preferred_element_type=F32) for z in zs]
    b_s = []
    for a in a_s:
        a = a.astype(BF16)
        ar, ai = a[:tr_rows], a[tr_rows:]
        br = ar * tr2 - ai * ti2
        bi = ar * ti2 + ai * tr2
        b_s.append(jnp.concatenate([
            jnp.concatenate([br[:, :LANES], bi[:, :LANES]], axis=1),
            jnp.concatenate([br[:, LANES:], bi[:, LANES:]], axis=1)], axis=0))
    return [jnp.dot(b, g_fwd, preferred_element_type=F32) for b in b_s]


def _cmul_spectrum(x, kr, ki):
    x = x.astype(BF16)
    xr, xi = x[:, :LANES], x[:, LANES:]
    return jnp.concatenate([xr * kr - xi * ki, xr * ki + xi * kr], axis=1)


def _fft_inv(ys, bd_inv, tr, ti, g_inv):
    tr_rows = ys[0].shape[0] // 2
    trs = jnp.concatenate([tr, tr], axis=0)
    tis = jnp.concatenate([ti, ti], axis=0)
    p_s = [jnp.dot(y, g_inv, preferred_element_type=F32) for y in ys]
    q_s = []
    for p in p_s:
        p = p.astype(BF16)
        pr, pi = p[:, :LANES], p[:, LANES:]
        qr = pr * trs + pi * tis
        qi = pi * trs - pr * tis
        q_s.append(jnp.concatenate([
            jnp.concatenate([qr[:tr_rows], qr[tr_rows:]], axis=1),
            jnp.concatenate([qi[:tr_rows], qi[tr_rows:]], axis=1)], axis=0))
    return [jnp.dot(bd_inv, q, preferred_element_type=F32) for q in q_s]


def _filter_mlp_kernel(zT_ref, t_ref, w1_ref, b1_ref, w2_ref, b2_ref, w3_ref, b3_ref,
                       fr_ref, w4_ref, ad_ref, o_ref, *, chunk):
    hp = lax.Precision.HIGHEST
    tl = zT_ref.shape[1]
    fr = fr_ref[...]
    h = jnp.sin(fr * (jnp.dot(w1_ref[...], zT_ref[...], precision=hp,
                              preferred_element_type=F32) + b1_ref[...]))
    h = jnp.sin(fr * (jnp.dot(w2_ref[...], h, precision=hp,
                              preferred_element_type=F32) + b2_ref[...]))
    h = jnp.sin(fr * (jnp.dot(w3_ref[...], h, precision=hp,
                              preferred_element_type=F32) + b3_ref[...]))
    h_hi = h.astype(BF16)
    h_lo = (h - h_hi.astype(F32)).astype(BF16)
    h3 = jnp.concatenate([h_hi, h_lo, h_hi], axis=0)
    t = t_ref[...]
    c = ad_ref.shape[0]
    first = (pl.program_id(0) == 0) & (lax.broadcasted_iota(jnp.int32, (chunk, tl), 1) == 0)
    for cc in range(c // chunk):
        ad = jnp.concatenate([ad_ref[cc * chunk:(cc + 1) * chunk, :]] * (tl // LANES), axis=1)
        decay = jnp.exp(-(t * ad))
        for od in range(2 * HY_ORDER):
            r0 = od * c + cc * chunk
            v = jnp.dot(w4_ref[r0:r0 + chunk, :], h3, preferred_element_type=F32) * decay
            if od % 2 == 1:
                v = jnp.where(first, 0.0, v)
            o_ref[r0:r0 + chunk, :] = v


def _filter_mlp(zT, tvec, w1T, b1, w2T, b2, w3T, b3, fr, w4T, ad, tl):
    nz, seq = zT.shape
    rows, hid = w4T.shape
    full = lambda a: pl.BlockSpec(a.shape, lambda i: (0,) * a.ndim)
    return pl.pallas_call(
        functools.partial(_filter_mlp_kernel, chunk=FILTER_CHUNK),
        grid=(seq // tl,),
        in_specs=[
            pl.BlockSpec((nz, tl), lambda i: (0, i)),
            pl.BlockSpec((1, tl), lambda i: (0, i)),
            full(w1T), full(b1), full(w2T), full(b2), full(w3T), full(b3), full(fr),
            full(w4T), full(ad),
        ],
        out_specs=pl.BlockSpec((rows, tl), lambda i: (0, i)),
        out_shape=jax.ShapeDtypeStruct((rows, seq), F32),
        compiler_params=_cparams(("arbitrary",)),
        name="filter_mlp",
    )(zT, tvec, w1T, b1, w2T, b2, w3T, b3, fr, w4T, ad)


def _filter_fft_kernel(hf_ref, hb_ref, bdf_ref, tw_ref, gf_ref, o_ref, *, strips_per_iter):
    nstrips = hf_ref.shape[0] // STRIP

    def body(it, carry):
        strips = [it * strips_per_iter + k for k in range(strips_per_iter)]
        tiles = []
        for s in strips:
            for ref in (hf_ref, hb_ref):
                tiles += _strip_to_tiles(ref[pl.ds(pl.multiple_of(s * STRIP, STRIP), STRIP), :])
        xs = _fft_fwd(tiles, bdf_ref[...], tw_ref[0], tw_ref[1], gf_ref[...])
        per_dir = len(tiles) // (2 * strips_per_iter)
        rows = xs[0].shape[0]
        for k, s in enumerate(strips):
            for t in range(per_dir):
                xf, xb = xs[2 * per_dir * k + t], xs[2 * per_dir * k + per_dir + t]
                o_ref[0, s, t * rows:(t + 1) * rows, :] = jnp.concatenate(
                    [xf[:, :LANES] + xb[:, :LANES], xf[:, LANES:] - xb[:, LANES:]],
                    axis=1).astype(o_ref.dtype)
        return carry

    lax.fori_loop(0, nstrips // strips_per_iter, body, 0)


def _filter_fft(hT, consts, c, r, cs):
    bd_fwd, _, tw, g_fwd, _ = consts
    seq = hT.shape[1]
    nblk = c // cs
    full = lambda a: pl.BlockSpec(a.shape, lambda o, j: (0,) * a.ndim)
    return pl.pallas_call(
        functools.partial(_filter_fft_kernel, strips_per_iter=FILTER_STRIPS_PER_ITER),
        grid=(HY_ORDER, nblk),
        in_specs=[
            pl.BlockSpec((cs, seq), lambda o, j: ((2 * o) * nblk + j, 0)),
            pl.BlockSpec((cs, seq), lambda o, j: ((2 * o + 1) * nblk + j, 0)),
            full(bd_fwd), full(tw), full(g_fwd),
        ],
        out_specs=pl.BlockSpec((1, cs // STRIP, STRIP * r, 2 * LANES),
                               lambda o, j: (o, j, 0, 0)),
        out_shape=jax.ShapeDtypeStruct((HY_ORDER, c // STRIP, STRIP * r, 2 * LANES), BF16),
        compiler_params=_cparams(("arbitrary", "arbitrary")),
        name="filter_fft",
    )(hT, hT, bd_fwd, tw, g_fwd)


def _hyena_kernel(v_ref, x1_ref, x2_ref, par_ref, kh_ref, bdf_ref, bdi_ref, tw_ref, gf_ref,
                  gi_ref, o_ref, *, strips_per_iter):
    seq = v_ref.shape[2]
    r = seq // LANES
    nstrips = v_ref.shape[1] // STRIP
    lane = lax.broadcasted_iota(jnp.int32, (SUBLANES, LANES), 1)

    def short_conv(p, w):
        p32 = pltpu.bitcast(p, jnp.uint32)
        prev = pltpu.roll(p32, 1, axis=1)
        nxt = pltpu.roll(p32, seq - 1, axis=1)
        zero = jnp.uint32(0)
        prev = jnp.concatenate(
            [jnp.where(lane == 0, zero, prev[:, :LANES]), prev[:, LANES:]], axis=1)
        nxt = jnp.concatenate(
            [nxt[:, :seq - LANES], jnp.where(lane == LANES - 1, zero, nxt[:, seq - LANES:])],
            axis=1)
        prev = pltpu.bitcast(prev, BF16)
        nxt = pltpu.bitcast(nxt, BF16)
        wide = lambda k: jnp.concatenate([w[k].astype(BF16)] * r, axis=1)
        u = prev * wide(0) + p * wide(1) + nxt * wide(2) + wide(3)
        return u.astype(F32)

    def body(it, carry):
        strips = [it * strips_per_iter + k for k in range(strips_per_iter)]
        rows = [pl.ds(pl.multiple_of(s * STRIP, STRIP), STRIP) for s in strips]
        tr, ti = tw_ref[0], tw_ref[1]
        def part(ref, k0):
            tiles = []
            for rw in rows:
                tiles += _strip_to_tiles(short_conv(
                    ref[0, rw, :], [par_ref[k0 + k, rw, :] for k in range(4)]))
            return tiles

        zs = part(v_ref, 0)
        per_strip = len(zs) // strips_per_iter
        spec_rows = 2 * zs[0].shape[0]
        gates = (x1_ref, x2_ref)
        for o in range(HY_ORDER):
            gs = part(gates[o], 4 + 4 * o)
            xs = _fft_fwd(zs, bdf_ref[...], tr, ti, gf_ref[...])
            ys = []
            for n, x in enumerate(xs):
                s, t = strips[n // per_strip], n % per_strip
                kh = kh_ref.at[o, s, t * spec_rows:(t + 1) * spec_rows]
                ys.append(_cmul_spectrum(x, kh[:, :LANES], kh[:, LANES:]))
            convs = _fft_inv(ys, bdi_ref[...], tr, ti, gi_ref[...])
            biases = []
            for rw in rows:
                biases += _row_param_tiles(par_ref[12 + o, rw, :], r)
            zs = [gate * (conv + z * bias)
                  for z, gate, conv, bias in zip(zs, gs, convs, biases)]
        for k, rw in enumerate(rows):
            o_ref[0, rw, :] = _tiles_to_strip(
                zs[k * per_strip:(k + 1) * per_strip]).astype(o_ref.dtype)
        return carry

    lax.fori_loop(0, nstrips // strips_per_iter, body, 0)


def _hyena(pT, par, khat, consts, batch, c, cs, strips_per_iter):
    bd_fwd, bd_inv, tw, g_fwd, g_inv = consts
    seq = pT.shape[2]
    nblk = c // cs
    full = lambda a: pl.BlockSpec(a.shape, lambda j, b: (0,) * a.ndim)
    part = lambda k: pl.BlockSpec((1, cs, seq), lambda j, b, k=k: (b, k * nblk + j, 0))
    return pl.pallas_call(
        functools.partial(_hyena_kernel, strips_per_iter=strips_per_iter),
        grid=(nblk, batch),
        in_specs=[
            part(0), part(1), part(2),
            pl.BlockSpec((par.shape[0], cs, LANES), lambda j, b: (0, j, 0)),
            pl.BlockSpec((HY_ORDER, cs // STRIP) + khat.shape[2:], lambda j, b: (0, j, 0, 0)),
            full(bd_fwd), full(bd_inv), full(tw), full(g_fwd), full(g_inv),
        ],
        out_specs=pl.BlockSpec((1, cs, seq), lambda j, b: (b, j, 0)),
        out_shape=jax.ShapeDtypeStruct((batch, c, seq), BF16),
        compiler_params=_cparams(("arbitrary", "arbitrary")),
        name="hyena",
    )(pT, pT, pT, par, khat, bd_fwd, bd_inv, tw, g_fwd, g_inv)


def _nat_kernel(q_ref, k_ref, v_ref, bias_ref, o_ref, *, rows, scale, rows_per_iter):
    npair = q_ref.shape[2] // LANES
    win = NAT_KH * GRID_W
    lo = lax.broadcasted_iota(jnp.int32, (GRID_W, LANES), 1) < LANES // 2

    def body(it, carry):
        chains = []
        for u in range(rows_per_iter):
            i = it * rows_per_iter + u
            rs = jnp.clip(i - NAT_KH // 2, 0, rows - NAT_KH)
            d = rs - i + (NAT_KH - 1)
            q0 = pl.multiple_of(i * GRID_W, GRID_W)
            k0 = pl.multiple_of(rs * GRID_W, GRID_W)
            for p in range(npair):
                chains.append((d, q0, k0, p, slice(p * LANES, (p + 1) * LANES)))
        scores = []
        for d, q0, k0, p, cols in chains:
            qp = q_ref[0, pl.ds(q0, GRID_W), cols].astype(F32) * scale
            q2 = jnp.concatenate([jnp.where(lo, qp, 0.0), jnp.where(lo, 0.0, qp)], axis=0)
            s = lax.dot_general(q2.astype(BF16), k_ref[0, pl.ds(k0, win), cols],
                                (((1,), (1,)), ((), ())), preferred_element_type=F32)
            scores.append(s + bias_ref[d, p])
        probs = []
        for s in scores:
            m = jnp.max(s, axis=-1, keepdims=True)
            pe = jnp.exp((s - m).astype(BF16))
            probs.append((pe, jnp.sum(pe.astype(F32), axis=-1, keepdims=True)))
        for (d, q0, k0, p, cols), (pe, l) in zip(chains, probs):
            o = jnp.dot(pe, v_ref[0, pl.ds(k0, win), cols], preferred_element_type=F32) / l
            o_ref[0, pl.ds(q0, GRID_W), cols] = jnp.where(
                lo, o[:GRID_W], o[GRID_W:]).astype(o_ref.dtype)
        return carry

    lax.fori_loop(0, rows // rows_per_iter, body, 0)


def _nat(pn3, bias, batch, seq, width, hd):
    gw = 2 * LANES
    ng = width // gw
    rows = seq // GRID_W
    blk = lambda k: pl.BlockSpec((1, seq, gw), lambda hg, b, k=k: (b, 0, k * ng + hg))
    return pl.pallas_call(
        functools.partial(_nat_kernel, rows=rows, scale=hd ** -0.5,
                          rows_per_iter=NAT_ROWS_PER_ITER),
        grid=(ng, batch),
        in_specs=[
            blk(0), blk(1), blk(2),
            pl.BlockSpec((NAT_KH, gw // LANES, 2 * GRID_W, NAT_KH * GRID_W),
                         lambda hg, b: (0, hg, 0, 0)),
        ],
        out_specs=pl.BlockSpec((1, seq, gw), lambda hg, b: (b, 0, hg)),
        out_shape=jax.ShapeDtypeStruct((batch, seq, width), BF16),
        compiler_params=_cparams(("arbitrary", "arbitrary")),
        name="nat",
    )(pn3, pn3, pn3, bias)


def _nat_bias_table(rpb):
    h = rpb.shape[0]
    cols = np.arange(GRID_W)
    col_start = np.clip(cols - NAT_KW // 2, 0, GRID_W - NAT_KW)
    cc = cols[None, :]
    valid = (cc >= col_start[:, None]) & (cc < col_start[:, None] + NAT_KW)
    col_off = cc - cols[:, None] + (NAT_KW - 1)
    pick = (col_off[None] == np.arange(2 * NAT_KW - 1)[:, None, None]) & valid[None]
    pick = jnp.asarray(pick.reshape(2 * NAT_KW - 1, -1), F32)
    nro = rpb.shape[1]
    t = jnp.dot(rpb.astype(F32).reshape(h * nro, -1), pick, precision=lax.Precision.HIGHEST)
    t = t.reshape(h, nro, GRID_W, GRID_W)
    t = jnp.where(valid[None, None], t, MASK_VALUE)
    t = jnp.stack([t[:, d:d + NAT_KH] for d in range(NAT_KH)], axis=0)
    t = jnp.transpose(t, (0, 1, 3, 2, 4))
    return t.reshape(NAT_KH, h // 2, 2 * GRID_W, NAT_KH * GRID_W)


def _out_proj_kernel(x_ref, yh_ref, yn_ref, gh_ref, gn_ref, woh_ref, won_ref, h_ref,
                     nh_ref, nn_ref):
    s = pl.program_id(0)
    tm = x_ref.shape[0]

    def normalise():
        yh = yh_ref[0].astype(F32)
        ms = jnp.mean(yh * yh, axis=0, keepdims=True)
        gh = jnp.concatenate([gh_ref[...]] * (tm // LANES), axis=1)
        nh_ref[s % 2] = (yh * lax.rsqrt(ms + NORM_EPS) * gh).astype(BF16)
        nn_ref[s % 2] = _rms(yn_ref[...].astype(F32), gn_ref[...]).astype(BF16)

    @pl.when(s == 0)
    def _():
        normalise()

    @pl.when(s > 0)
    def _():
        acc = lax.dot_general(nh_ref[1 - s % 2], woh_ref[...], (((0,), (0,)), ((), ())),
                              preferred_element_type=F32)
        acc = acc + jnp.dot(nn_ref[1 - s % 2], won_ref[...], preferred_element_type=F32)
        h_ref[...] = x_ref[...] + acc
        normalise()


def _out_proj(x2, yhT, yn, gh, gn, woh, won, seq, tm):
    ntok, d = x2.shape
    c = yn.shape[1]
    tpb = seq // tm
    nt = ntok // tm
    tile = lambda s: jnp.maximum(s - 1, 0)
    nxt = lambda s: jnp.minimum(s, nt - 1)
    full = lambda a: pl.BlockSpec(a.shape, lambda s: (0,) * a.ndim)
    resident = lambda a: pl.BlockSpec(a.shape, lambda s: (0,) * a.ndim,
                                      pipeline_mode=pl.Buffered(1))
    return pl.pallas_call(
        _out_proj_kernel,
        grid=(nt + 1,),
        in_specs=[
            pl.BlockSpec((tm, d), lambda s: (tile(s), 0)),
            pl.BlockSpec((1, c, tm), lambda s: (nxt(s) // tpb, 0, nxt(s) % tpb)),
            pl.BlockSpec((tm, c), lambda s: (nxt(s), 0)),
            full(gh), full(gn), resident(woh), resident(won),
        ],
        out_specs=pl.BlockSpec((tm, d), lambda s: (tile(s), 0)),
        out_shape=jax.ShapeDtypeStruct((ntok, d), F32),
        scratch_shapes=[pltpu.VMEM((2, c, tm), BF16), pltpu.VMEM((2, tm, c), BF16)],
        compiler_params=_cparams(("arbitrary",)),
        name="out_proj",
    )(x2, yhT, yn, gh, gn, woh, won)


def _mlp_kernel(h_ref, g_ref, wu_ref, wd_ref, gf_ref, o_ref, a_ref):
    i, j = pl.program_id(0), pl.program_id(1)
    last = pl.num_programs(1) - 1
    slot = i % 2

    @pl.when((i == 0) & (j == 0))
    def _():
        a_ref[0] = _rms(h_ref[...], g_ref[...]).astype(BF16)

    @pl.when(j == 0)
    def _():
        o_ref[...] = h_ref[...]

    def hidden_tile():
        u = jnp.dot(a_ref[slot], wu_ref[...], preferred_element_type=F32)
        u = jnp.square(jnp.maximum(u, 0.0)).astype(BF16)
        return o_ref[...] + jnp.dot(u, wd_ref[...], preferred_element_type=F32)

    @pl.when(j < last)
    def _():
        o_ref[...] = hidden_tile()

    @pl.when(j == last)
    def _():
        acc = hidden_tile()
        a_ref[1 - slot] = _rms(h_ref[...], g_ref[...]).astype(BF16)
        o_ref[...] = _rms(acc, gf_ref[...])


def _mlp(h, g, wu, wd, gf, tm, tf):
    ntok, d = h.shape
    dff = wu.shape[1]
    nt, nf = ntok // tm, dff // tf
    assert nf >= 2
    h_map = lambda i, j: (jnp.minimum(i + (j >= nf // 2).astype(jnp.int32), nt - 1), 0)
    return pl.pallas_call(
        _mlp_kernel,
        grid=(nt, nf),
        in_specs=[
            pl.BlockSpec((tm, d), h_map),
            pl.BlockSpec((1, d), lambda i, j: (0, 0)),
            pl.BlockSpec((d, tf), lambda i, j: (0, j)),
            pl.BlockSpec((tf, d), lambda i, j: (j, 0)),
            pl.BlockSpec((1, d), lambda i, j: (0, 0)),
        ],
        out_specs=pl.BlockSpec((tm, d), lambda i, j: (i, 0)),
        out_shape=jax.ShapeDtypeStruct((ntok, d), F32),
        scratch_shapes=[pltpu.VMEM((2, tm, d), BF16)],
        compiler_params=_cparams(("arbitrary", "arbitrary")),
        name="mlp",
    )(h, g, wu, wd, gf)


def _positional_features(seq, emb):
    bands = (emb - 1) // 2
    t = np.linspace(0.0, 1.0, seq)
    w_ang = (2.0 * np.pi / seq) * np.arange(seq)
    f = np.linspace(1e-4, bands - 1, bands)
    ang = w_ang[None, :] * f[:, None]
    z = np.concatenate([t[None, :], np.cos(ang), -np.sin(ang)], axis=0)
    pad = (-z.shape[0]) % SUBLANES
    z = np.concatenate([z, np.zeros((pad, seq))], axis=0)
    return jnp.asarray(z, F32), jnp.asarray(t[None, :], F32)


def _lane_rep(v):
    return jnp.broadcast_to(v.astype(F32)[..., None], v.shape + (LANES,))


def _trunk(x, w):
    batch, seq, d = x.shape
    c = w["hy_bias"].shape[-1]
    r = seq // LANES
    rows = seq // GRID_W
    assert seq % LANES == 0 and c % STRIP == 0
    assert rows >= NAT_KH and seq % GRID_W == 0
    consts = _fft_constants(r)
    cs = HYENA_CS

    emb = w["hy_pe_w1"].shape[0]
    zT, tvec = _positional_features(seq, emb)
    w1T = jnp.pad(w["hy_pe_w1"].astype(F32).T, ((0, 0), (0, zT.shape[0] - emb)))
    col = lambda v: v.astype(F32)[:, None]
    max_decay = math.log(HY_DECAY_TARGET) / HY_FAST_DECAY
    min_decay = math.log(HY_DECAY_TARGET) / HY_SLOW_DECAY
    absd = jnp.asarray(np.abs(np.linspace(min_decay, max_decay, c)), F32)
    w4T = w["hy_pe_w4"].astype(F32).T
    hi_bits = lax.bitcast_convert_type(w4T, jnp.uint32) & jnp.uint32(0xFFFF0000)
    w4_hi_f32 = lax.bitcast_convert_type(hi_bits, F32)
    w4_hi = w4_hi_f32.astype(BF16)
    w4_lo = (w4T - w4_hi_f32).astype(BF16)
    hT = _filter_mlp(zT, tvec, w1T, col(w["hy_pe_b1"]), w["hy_pe_w2"].astype(F32).T,
                     col(w["hy_pe_b2"]), w["hy_pe_w3"].astype(F32).T, col(w["hy_pe_b3"]),
                     col(w["hy_pe_freq"]), jnp.concatenate([w4_hi, w4_hi, w4_lo], axis=1),
                     _lane_rep(absd), tl=FILTER_TL)
    khat = _filter_fft(hT, consts, c, r, cs)

    x2 = x.reshape(batch * seq, d)
    w_in = w["w_in"]
    n_hy = 3 * c
    whT = w_in[:, :n_hy].T.astype(BF16)
    wn = w_in[:, n_hy:].astype(BF16)
    pT, pn = _in_proj(x2, w["norm_mix_g"].astype(F32)[None, :], whT, wn, batch, seq,
                      tm=PROJ_TM)

    cw = w["hy_conv_w"].astype(F32).reshape(3, 3, c)
    cb = w["hy_conv_b"].astype(F32).reshape(3, c)
    par = [jnp.concatenate([cw[:, p], cb[p][None]], axis=0) for p in range(3)]
    par = _lane_rep(jnp.concatenate(par + [w["hy_bias"].astype(F32)], axis=0))
    yhT = _hyena(pT, par, khat, consts, batch, c, cs, strips_per_iter=4 if r <= 32 else 2)

    nat_w = pn.shape[1] // 3
    bias = _nat_bias_table(w["nat_rpb"])
    yn = _nat(pn.reshape(batch, seq, 3 * nat_w), bias, batch, seq, nat_w, nat_w // NAT_HEADS)
    yn = yn.reshape(batch * seq, nat_w)

    w_out = w["w_out"].astype(BF16)
    h = _out_proj(x2, yhT, yn, _lane_rep(w["gnorm_hy"]), w["gnorm_nat"].astype(F32)[None, :],
                  w_out[:c], w_out[c:], seq, tm=PROJ_TM)
    y = _mlp(h, w["norm_mlp_g"].astype(F32)[None, :], w["w_up"].astype(BF16),
             w["w_down"].astype(BF16), w["norm_f_g"].astype(F32)[None, :],
             tm=PROJ_TM, tf=MLP_TF)
    return y.reshape(batch, seq, d)


def kernel(x_prompt, x_sample, norm_mix_g, w_in, hy_conv_w, hy_conv_b, hy_pe_w1, hy_pe_b1,
           hy_pe_w2, hy_pe_b2, hy_pe_w3, hy_pe_b3, hy_pe_freq, hy_pe_w4, hy_bias, nat_rpb,
           gnorm_hy, gnorm_nat, w_out, norm_mlp_g, w_up, w_down, norm_f_g):
    assert norm_mix_g.shape[0] == 1, "single-layer trunk"
    w = dict(norm_mix_g=norm_mix_g[0], w_in=w_in[0], hy_conv_w=hy_conv_w[0],
             hy_conv_b=hy_conv_b[0], hy_pe_w1=hy_pe_w1[0], hy_pe_b1=hy_pe_b1[0],
             hy_pe_w2=hy_pe_w2[0], hy_pe_b2=hy_pe_b2[0], hy_pe_w3=hy_pe_w3[0],
             hy_pe_b3=hy_pe_b3[0], hy_pe_freq=hy_pe_freq[0], hy_pe_w4=hy_pe_w4[0],
             hy_bias=hy_bias[0], nat_rpb=nat_rpb[0], gnorm_hy=gnorm_hy[0],
             gnorm_nat=gnorm_nat[0], w_out=w_out[0], norm_mlp_g=norm_mlp_g[0],
             w_up=w_up[0], w_down=w_down[0], norm_f_g=norm_f_g)
    return (_trunk(x_prompt, w), _trunk(x_sample, w))
```

```python
import functools
import math

import numpy as np
import jax
import jax.numpy as jnp
from jax import lax
from jax.experimental import pallas as pl
from jax.experimental.pallas import tpu as pltpu

F32 = jnp.float32
BF16 = jnp.bfloat16

LANES = 128
SUBLANES = 8
STRIP = 2 * SUBLANES
MXU_DIM = 256
VMEM_LIMIT = 58 * 1024 * 1024
MLP_VMEM_LIMIT = 61 * 1024 * 1024

PROJ_TM = 512
MLP_TF = 2048
FILTER_TL = 512
FILTER_CHUNK = 512
HYENA_CS = 128
FILTER_STRIPS_PER_ITER = 2
NAT_ROWS_PER_ITER = 8

NORM_EPS = 1e-5
GRID_W = 64
NAT_HEADS = 16
NAT_KH = 8
NAT_KW = 16
HY_ORDER = 2
HY_FAST_DECAY = 0.3
HY_SLOW_DECAY = 1.5
HY_DECAY_TARGET = 1e-2
MASK_VALUE = -1e30


def _cparams(sem):
    return pltpu.CompilerParams(dimension_semantics=sem, vmem_limit_bytes=VMEM_LIMIT)


def _rms(x, g):
    ms = jnp.mean(x * x, axis=-1, keepdims=True)
    return x * lax.rsqrt(ms + NORM_EPS) * g


def _in_proj_kernel(x_ref, g_ref, whT_ref, wn_ref, pT_ref, pn_ref, a_ref):
    s = pl.program_id(0)

    def normalise():
        a_ref[s % 2] = _rms(x_ref[...], g_ref[...]).astype(BF16)

    @pl.when(s == 0)
    def _():
        normalise()

    @pl.when(s > 0)
    def _():
        a = a_ref[1 - s % 2]
        pT_ref[0] = lax.dot_general(whT_ref[...], a, (((1,), (1,)), ((), ())),
                                    preferred_element_type=F32).astype(BF16)
        pn_ref[...] = jnp.dot(a, wn_ref[...], preferred_element_type=F32).astype(BF16)
        normalise()


def _in_proj(x2, g, whT, wn, batch, seq, tm):
    ntok, d = x2.shape
    tpb = seq // tm
    nt = ntok // tm
    tile = lambda s: jnp.maximum(s - 1, 0)
    resident = lambda a: pl.BlockSpec(a.shape, lambda s: (0,) * a.ndim,
                                      pipeline_mode=pl.Buffered(1))
    return pl.pallas_call(
        _in_proj_kernel,
        grid=(nt + 1,),
        in_specs=[
            pl.BlockSpec((tm, d), lambda s: (jnp.minimum(s, nt - 1), 0)),
            pl.BlockSpec((1, d), lambda s: (0, 0)),
            resident(whT), resident(wn),
        ],
        out_specs=[
            pl.BlockSpec((1, whT.shape[0], tm), lambda s: (tile(s) // tpb, 0, tile(s) % tpb)),
            pl.BlockSpec((tm, wn.shape[1]), lambda s: (tile(s), 0)),
        ],
        out_shape=[
            jax.ShapeDtypeStruct((batch, whT.shape[0], seq), BF16),
            jax.ShapeDtypeStruct((ntok, wn.shape[1]), BF16),
        ],
        scratch_shapes=[pltpu.VMEM((2, tm, d), BF16)],
        compiler_params=_cparams(("arbitrary",)),
        name="in_proj",
    )(x2, g, whT, wn)


def _tile_channels(r):
    return SUBLANES if SUBLANES * r <= MXU_DIM else SUBLANES // 2


def _fft_constants(r):
    g = _tile_channels(r)
    n2f = 2 * r
    m = LANES * n2f
    k2 = np.arange(r, dtype=np.float64)[:, None]
    n2 = np.arange(r, dtype=np.float64)[None, :]
    th = 2.0 * np.pi * n2 * (k2 + 0.5) / n2f
    er, ei = np.cos(th), -np.sin(th)
    eye = np.eye(g)
    bd_fwd = np.concatenate([np.kron(er, eye), np.kron(ei, eye)], axis=0)
    bd_inv = np.concatenate([np.kron(er.T, eye), np.kron(ei.T, eye)], axis=1) * (2.0 / m)
    n1 = np.arange(LANES, dtype=np.float64)[None, :]
    ph = 2.0 * np.pi * n1 * (k2 + 0.5) / m
    tw = np.stack([np.repeat(np.cos(ph), g, axis=0), np.repeat(-np.sin(ph), g, axis=0)])
    a = np.arange(LANES, dtype=np.float64)
    ps = 2.0 * np.pi * np.outer(a, a) / LANES
    gr, gi = np.cos(ps), -np.sin(ps)
    g_fwd = np.block([[gr, gi], [-gi, gr]])
    g_inv = np.block([[gr, -gi], [gi, gr]])
    as_bf = lambda v: jnp.asarray(v, F32).astype(BF16)
    return as_bf(bd_fwd), as_bf(bd_inv), as_bf(tw), as_bf(g_fwd), as_bf(g_inv)


def _swap_halves(a, b):
    low = lax.broadcasted_iota(jnp.int32, (SUBLANES, LANES), 0) < SUBLANES // 2
    rot = lambda v: pltpu.roll(v, SUBLANES // 2, axis=0)
    return jnp.where(low, a, rot(b)), jnp.where(low, rot(a), b)


def _strip_to_tiles(s):
    r = s.shape[1] // LANES
    blk = lambda h, n: s[h * SUBLANES:(h + 1) * SUBLANES, n * LANES:(n + 1) * LANES]
    if _tile_channels(r) == SUBLANES:
        return [jnp.concatenate(
            [jnp.concatenate([blk(0, n), blk(1, n)], axis=1) for n in range(r)], axis=0)]
    tiles = []
    for h in range(STRIP // SUBLANES):
        rows = [jnp.concatenate(_swap_halves(blk(h, n), blk(h, n + 1)), axis=1)
                for n in range(0, r, 2)]
        tiles.append(jnp.concatenate(rows, axis=0))
    return tiles


def _tiles_to_strip(tiles):
    if len(tiles) == 1:
        z = tiles[0]
        r = z.shape[0] // SUBLANES
        return jnp.concatenate(
            [jnp.concatenate([z[n * SUBLANES:(n + 1) * SUBLANES, :LANES],
                              z[n * SUBLANES:(n + 1) * SUBLANES, LANES:]], axis=0)
             for n in range(r)], axis=1)
    halves = []
    for z in tiles:
        blocks = []
        for k in range(z.shape[0] // SUBLANES):
            v = z[k * SUBLANES:(k + 1) * SUBLANES]
            blocks.extend(_swap_halves(v[:, :LANES], v[:, LANES:]))
        halves.append(jnp.concatenate(blocks, axis=1))
    return jnp.concatenate(halves, axis=0)


def _row_param_tiles(b, r):
    reps = _tile_channels(r) * r // SUBLANES
    if _tile_channels(r) == SUBLANES:
        return [jnp.concatenate([jnp.concatenate([b[:SUBLANES], b[SUBLANES:]], axis=1)] * reps,
                                axis=0)]
    tiles = []
    for h in range(STRIP // SUBLANES):
        v = b[h * SUBLANES:(h + 1) * SUBLANES]
        tiles.append(jnp.concatenate([jnp.concatenate(_swap_halves(v, v), axis=1)] * reps,
                                     axis=0))
    return tiles


def _fft_fwd(zs, bd_fwd, tr, ti, g_fwd):
    tr_rows = zs[0].shape[0]
    tr2 = jnp.concatenate([tr, tr], axis=1)
    ti2 = jnp.concatenate([ti, ti], axis=1)
    a_s = [jnp.dot(bd_fwd, z.astype(BF16), preferred_element_type=F32) for z in zs]
    b_s = []
    for a in a_s:
        a = a.astype(BF16)
        ar, ai = a[:tr_rows], a[tr_rows:]
        br = ar * tr2 - ai * ti2
        bi = ar * ti2 + ai * tr2
        b_s.append(jnp.concatenate([
            jnp.concatenate([br[:, :LANES], bi[:, :LANES]], axis=1),
            jnp.concatenate([br[:, LANES:], bi[:, LANES:]], axis=1)], axis=0))
    return [jnp.dot(b, g_fwd, preferred_element_type=F32) for b in b_s]


def _cmul_spectrum(x, kr, ki):
    x = x.astype(BF16)
    xr, xi = x[:, :LANES], x[:, LANES:]
    return jnp.concatenate([xr * kr - xi * ki, xr * ki + xi * kr], axis=1)


def _fft_inv(ys, bd_inv, tr, ti, g_inv):
    tr_rows = ys[0].shape[0] // 2
    trs = jnp.concatenate([tr, tr], axis=0)
    tis = jnp.concatenate([ti, ti], axis=0)
    p_s = [jnp.dot(y, g_inv, preferred_element_type=F32) for y in ys]
    q_s = []
    for p in p_s:
        p = p.astype(BF16)
        pr, pi = p[:, :LANES], p[:, LANES:]
        qr = pr * trs + pi * tis
        qi = pi * trs - pr * tis
        q_s.append(jnp.concatenate([
            jnp.concatenate([qr[:tr_rows], qr[tr_rows:]], axis=1),
            jnp.concatenate([qi[:tr_rows], qi[tr_rows:]], axis=1)], axis=0))
    return [jnp.dot(bd_inv, q, preferred_element_type=F32) for q in q_s]


def _filter_mlp_kernel(zT_ref, t_ref, w1_ref, b1_ref, w2_ref, b2_ref, w3_ref, b3_ref,
                       fr_ref, w4_ref, ad_ref, o_ref, *, chunk):
    hp = lax.Precision.HIGHEST
    tl = zT_ref.shape[1]
    fr = fr_ref[...]
    h = jnp.sin(fr * (jnp.dot(w1_ref[...], zT_ref[...], precision=hp,
                              preferred_element_type=F32) + b1_ref[...]))
    h = jnp.sin(fr * (jnp.dot(w2_ref[...], h, precision=hp,
                              preferred_element_type=F32) + b2_ref[...]))
    h = jnp.sin(fr * (jnp.dot(w3_ref[...], h, precision=hp,
                              preferred_element_type=F32) + b3_ref[...]))
    h_hi = h.astype(BF16)
    h_lo = (h - h_hi.astype(F32)).astype(BF16)
    h3 = jnp.concatenate([h_hi, h_lo, h_hi], axis=0)
    t = t_ref[...]
    c = ad_ref.shape[0]
    first = (pl.program_id(0) == 0) & (lax.broadcasted_iota(jnp.int32, (chunk, tl), 1) == 0)
    for cc in range(c // chunk):
        ad = jnp.concatenate([ad_ref[cc * chunk:(cc + 1) * chunk, :]] * (tl // LANES), axis=1)
        decay = jnp.exp(-(t * ad))
        for od in range(2 * HY_ORDER):
            r0 = od * c + cc * chunk
            v = jnp.dot(w4_ref[r0:r0 + chunk, :], h3, preferred_element_type=F32) * decay
            if od % 2 == 1:
                v = jnp.where(first, 0.0, v)
            o_ref[r0:r0 + chunk, :] = v


def _filter_mlp(zT, tvec, w1T, b1, w2T, b2, w3T, b3, fr, w4T, ad, tl):
    nz, seq = zT.shape
    rows, hid = w4T.shape
    full = lambda a: pl.BlockSpec(a.shape, lambda i: (0,) * a.ndim)
    return pl.pallas_call(
        functools.partial(_filter_mlp_kernel, chunk=FILTER_CHUNK),
        grid=(seq // tl,),
        in_specs=[
            pl.BlockSpec((nz, tl), lambda i: (0, i)),
            pl.BlockSpec((1, tl), lambda i: (0, i)),
            full(w1T), full(b1), full(w2T), full(b2), full(w3T), full(b3), full(fr),
            full(w4T), full(ad),
        ],
        out_specs=pl.BlockSpec((rows, tl), lambda i: (0, i)),
        out_shape=jax.ShapeDtypeStruct((rows, seq), F32),
        compiler_params=_cparams(("arbitrary",)),
        name="filter_mlp",
    )(zT, tvec, w1T, b1, w2T, b2, w3T, b3, fr, w4T, ad)


def _filter_fft_kernel(hf_ref, hb_ref, bdf_ref, tw_ref, gf_ref, o_ref, *, strips_per_iter):
    nstrips = hf_ref.shape[0] // STRIP

    def body(it, carry):
        strips = [it * strips_per_iter + k for k in range(strips_per_iter)]
        tiles = []
        for s in strips:
            for ref in (hf_ref, hb_ref):
                tiles += _strip_to_tiles(ref[pl.ds(pl.multiple_of(s * STRIP, STRIP), STRIP), :])
        xs = _fft_fwd(tiles, bdf_ref[...], tw_ref[0], tw_ref[1], gf_ref[...])
        per_dir = len(tiles) // (2 * strips_per_iter)
        rows = xs[0].shape[0]
        for k, s in enumerate(strips):
            for t in range(per_dir):
                xf, xb = xs[2 * per_dir * k + t], xs[2 * per_dir * k + per_dir + t]
                o_ref[0, s, t * rows:(t + 1) * rows, :] = jnp.concatenate(
                    [xf[:, :LANES] + xb[:, :LANES], xf[:, LANES:] - xb[:, LANES:]],
                    axis=1).astype(o_ref.dtype)
        return carry

    lax.fori_loop(0, nstrips // strips_per_iter, body, 0)


def _filter_fft(hT, consts, c, r, cs):
    bd_fwd, _, tw, g_fwd, _ = consts
    seq = hT.shape[1]
    nblk = c // cs
    full = lambda a: pl.BlockSpec(a.shape, lambda o, j: (0,) * a.ndim)
    return pl.pallas_call(
        functools.partial(_filter_fft_kernel, strips_per_iter=FILTER_STRIPS_PER_ITER),
        grid=(HY_ORDER, nblk),
        in_specs=[
            pl.BlockSpec((cs, seq), lambda o, j: ((2 * o) * nblk + j, 0)),
            pl.BlockSpec((cs, seq), lambda o, j: ((2 * o + 1) * nblk + j, 0)),
            full(bd_fwd), full(tw), full(g_fwd),
        ],
        out_specs=pl.BlockSpec((1, cs // STRIP, STRIP * r, 2 * LANES),
                               lambda o, j: (o, j, 0, 0)),
        out_shape=jax.ShapeDtypeStruct((HY_ORDER, c // STRIP, STRIP * r, 2 * LANES), BF16),
        compiler_params=_cparams(("arbitrary", "arbitrary")),
        name="filter_fft",
    )(hT, hT, bd_fwd, tw, g_fwd)


def _hyena_kernel(v_ref, x1_ref, x2_ref, par_ref, kh_ref, bdf_ref, bdi_ref, tw_ref, gf_ref,
                  gi_ref, o_ref, *, strips_per_iter):
    seq = v_ref.shape[2]
    r = seq // LANES
    nstrips = v_ref.shape[1] // STRIP
    lane = lax.broadcasted_iota(jnp.int32, (SUBLANES, LANES), 1)

    def short_conv(p, w):
        p32 = pltpu.bitcast(p, jnp.uint32)
        prev = pltpu.roll(p32, 1, axis=1)
        nxt = pltpu.roll(p32, seq - 1, axis=1)
        zero = jnp.uint32(0)
        prev = jnp.concatenate(
            [jnp.where(lane == 0, zero, prev[:, :LANES]), prev[:, LANES:]], axis=1)
        nxt = jnp.concatenate(
            [nxt[:, :seq - LANES], jnp.where(lane == LANES - 1, zero, nxt[:, seq - LANES:])],
            axis=1)
        prev = pltpu.bitcast(prev, BF16)
        nxt = pltpu.bitcast(nxt, BF16)
        wide = lambda k: jnp.concatenate([w[k].astype(BF16)] * r, axis=1)
        u = prev * wide(0) + p * wide(1) + nxt * wide(2) + wide(3)
        return u.astype(F32)

    def body(it, carry):
        strips = [it * strips_per_iter + k for k in range(strips_per_iter)]
        rows = [pl.ds(pl.multiple_of(s * STRIP, STRIP), STRIP) for s in strips]
        tr, ti = tw_ref[0], tw_ref[1]
        def part(ref, k0):
            tiles = []
            for rw in rows:
                tiles += _strip_to_tiles(short_conv(
                    ref[0, rw, :], [par_ref[k0 + k, rw, :] for k in range(4)]))
            return tiles

        zs = part(v_ref, 0)
        per_strip = len(zs) // strips_per_iter
        spec_rows = 2 * zs[0].shape[0]
        gates = (x1_ref, x2_ref)
        for o in range(HY_ORDER):
            gs = part(gates[o], 4 + 4 * o)
            xs = _fft_fwd(zs, bdf_ref[...], tr, ti, gf_ref[...])
            ys = []
            for n, x in enumerate(xs):
                s, t = strips[n // per_strip], n % per_strip
                kh = kh_ref.at[o, s, t * spec_rows:(t + 1) * spec_rows]
                ys.append(_cmul_spectrum(x, kh[:, :LANES], kh[:, LANES:]))
            convs = _fft_inv(ys, bdi_ref[...], tr, ti, gi_ref[...])
            biases = []
            for rw in rows:
                biases += _row_param_tiles(par_ref[12 + o, rw, :], r)
            zs = [gate * (conv + z * bias)
                  for z, gate, conv, bias in zip(zs, gs, convs, biases)]
        for k, rw in enumerate(rows):
            o_ref[0, rw, :] = _tiles_to_strip(
                zs[k * per_strip:(k + 1) * per_strip]).astype(o_ref.dtype)
        return carry

    lax.fori_loop(0, nstrips // strips_per_iter, body, 0)


def _hyena(pT, par, khat, consts, batch, c, cs, strips_per_iter):
    bd_fwd, bd_inv, tw, g_fwd, g_inv = consts
    seq = pT.shape[2]
    nblk = c // cs
    full = lambda a: pl.BlockSpec(a.shape, lambda j, b: (0,) * a.ndim)
    part = lambda k: pl.BlockSpec((1, cs, seq), lambda j, b, k=k: (b, k * nblk + j, 0))
    return pl.pallas_call(
        functools.partial(_hyena_kernel, strips_per_iter=strips_per_iter),
        grid=(nblk, batch),
        in_specs=[
            part(0), part(1), part(2),
            pl.BlockSpec((par.shape[0], cs, LANES), lambda j, b: (0, j, 0)),
            pl.BlockSpec((HY_ORDER, cs // STRIP) + khat.shape[2:], lambda j, b: (0, j, 0, 0)),
            full(bd_fwd), full(bd_inv), full(tw), full(g_fwd), full(g_inv),
        ],
        out_specs=pl.BlockSpec((1, cs, seq), lambda j, b: (b, j, 0)),
        out_shape=jax.ShapeDtypeStruct((batch, c, seq), BF16),
        compiler_params=_cparams(("arbitrary", "arbitrary")),
        name="hyena",
    )(pT, pT, pT, par, khat, bd_fwd, bd_inv, tw, g_fwd, g_inv)


def _nat_kernel(q_ref, k_ref, v_ref, bias_ref, o_ref, *, rows, scale, rows_per_iter):
    npair = q_ref.shape[2] // LANES
    win = NAT_KH * GRID_W
    lo = lax.broadcasted_iota(jnp.int32, (GRID_W, LANES), 1) < LANES // 2

    def body(it, carry):
        chains = []
        for u in range(rows_per_iter):
            i = it * rows_per_iter + u
            rs = jnp.clip(i - NAT_KH // 2, 0, rows - NAT_KH)
            d = rs - i + (NAT_KH - 1)
            q0 = pl.multiple_of(i * GRID_W, GRID_W)
            k0 = pl.multiple_of(rs * GRID_W, GRID_W)
            for p in range(npair):
                chains.append((d, q0, k0, p, slice(p * LANES, (p + 1) * LANES)))
        scores = []
        for d, q0, k0, p, cols in chains:
            qp = q_ref[0, pl.ds(q0, GRID_W), cols].astype(F32) * scale
            q2 = jnp.concatenate([jnp.where(lo, qp, 0.0), jnp.where(lo, 0.0, qp)], axis=0)
            s = lax.dot_general(q2.astype(BF16), k_ref[0, pl.ds(k0, win), cols],
                                (((1,), (1,)), ((), ())), preferred_element_type=F32)
            scores.append(s + bias_ref[d, p])
        probs = []
        for s in scores:
            m = jnp.max(s, axis=-1, keepdims=True)
            pe = jnp.exp((s - m).astype(BF16))
            probs.append((pe, jnp.sum(pe.astype(F32), axis=-1, keepdims=True)))
        for (d, q0, k0, p, cols), (pe, l) in zip(chains, probs):
            o = jnp.dot(pe, v_ref[0, pl.ds(k0, win), cols], preferred_element_type=F32) / l
            o_ref[0, pl.ds(q0, GRID_W), cols] = jnp.where(
                lo, o[:GRID_W], o[GRID_W:]).astype(o_ref.dtype)
        return carry

    lax.fori_loop(0, rows // rows_per_iter, body, 0)


def _nat(pn3, bias, batch, seq, width, hd):
    gw = LANES
    ng = width // gw
    rows = seq // GRID_W
    blk = lambda k: pl.BlockSpec((1, seq, gw), lambda hg, b, k=k: (b, 0, k * ng + hg))
    return pl.pallas_call(
        functools.partial(_nat_kernel, rows=rows, scale=hd ** -0.5,
                          rows_per_iter=NAT_ROWS_PER_ITER),
        grid=(ng, batch),
        in_specs=[
            blk(0), blk(1), blk(2),
            pl.BlockSpec((NAT_KH, gw // LANES, 2 * GRID_W, NAT_KH * GRID_W),
                         lambda hg, b: (0, hg, 0, 0)),
        ],
        out_specs=pl.BlockSpec((1, seq, gw), lambda hg, b: (b, 0, hg)),
        out_shape=jax.ShapeDtypeStruct((batch, seq, width), BF16),
        compiler_params=_cparams(("arbitrary", "arbitrary")),
        name="nat",
    )(pn3, pn3, pn3, bias)


def _nat_bias_table(rpb):
    h = rpb.shape[0]
    cols = np.arange(GRID_W)
    col_start = np.clip(cols - NAT_KW // 2, 0, GRID_W - NAT_KW)
    cc = cols[None, :]
    valid = (cc >= col_start[:, None]) & (cc < col_start[:, None] + NAT_KW)
    col_off = cc - cols[:, None] + (NAT_KW - 1)
    pick = (col_off[None] == np.arange(2 * NAT_KW - 1)[:, None, None]) & valid[None]
    pick = jnp.asarray(pick.reshape(2 * NAT_KW - 1, -1), F32)
    nro = rpb.shape[1]
    t = jnp.dot(rpb.astype(F32).reshape(h * nro, -1), pick, precision=lax.Precision.HIGHEST)
    t = t.reshape(h, nro, GRID_W, GRID_W)
    t = jnp.where(valid[None, None], t, MASK_VALUE)
    t = jnp.stack([t[:, d:d + NAT_KH] for d in range(NAT_KH)], axis=0)
    t = jnp.transpose(t, (0, 1, 3, 2, 4))
    return t.reshape(NAT_KH, h // 2, 2 * GRID_W, NAT_KH * GRID_W)


def _out_proj_kernel(x_ref, yh_ref, yn_ref, gh_ref, gn_ref, woh_ref, won_ref, h_ref,
                     nh_ref, nn_ref):
    s = pl.program_id(0)
    tm = x_ref.shape[0]

    def normalise():
        yh = yh_ref[0].astype(F32)
        ms = jnp.mean(yh * yh, axis=0, keepdims=True)
        gh = jnp.concatenate([gh_ref[...]] * (tm // LANES), axis=1)
        nh_ref[s % 2] = (yh * lax.rsqrt(ms + NORM_EPS) * gh).astype(BF16)
        nn_ref[s % 2] = _rms(yn_ref[...].astype(F32), gn_ref[...]).astype(BF16)

    @pl.when(s == 0)
    def _():
        normalise()

    @pl.when(s > 0)
    def _():
        acc = lax.dot_general(nh_ref[1 - s % 2], woh_ref[...], (((0,), (0,)), ((), ())),
                              preferred_element_type=F32)
        acc = acc + jnp.dot(nn_ref[1 - s % 2], won_ref[...], preferred_element_type=F32)
        h_ref[...] = x_ref[...] + acc
        normalise()


def _out_proj(x2, yhT, yn, gh, gn, woh, won, seq, tm):
    ntok, d = x2.shape
    c = yn.shape[1]
    tpb = seq // tm
    nt = ntok // tm
    tile = lambda s: jnp.maximum(s - 1, 0)
    nxt = lambda s: jnp.minimum(s, nt - 1)
    full = lambda a: pl.BlockSpec(a.shape, lambda s: (0,) * a.ndim)
    resident = lambda a: pl.BlockSpec(a.shape, lambda s: (0,) * a.ndim,
                                      pipeline_mode=pl.Buffered(1))
    return pl.pallas_call(
        _out_proj_kernel,
        grid=(nt + 1,),
        in_specs=[
            pl.BlockSpec((tm, d), lambda s: (tile(s), 0)),
            pl.BlockSpec((1, c, tm), lambda s: (nxt(s) // tpb, 0, nxt(s) % tpb)),
            pl.BlockSpec((tm, c), lambda s: (nxt(s), 0)),
            full(gh), full(gn), resident(woh), resident(won),
        ],
        out_specs=pl.BlockSpec((tm, d), lambda s: (tile(s), 0)),
        out_shape=jax.ShapeDtypeStruct((ntok, d), F32),
        scratch_shapes=[pltpu.VMEM((2, c, tm), BF16), pltpu.VMEM((2, tm, c), BF16)],
        compiler_params=_cparams(("arbitrary",)),
        name="out_proj",
    )(x2, yhT, yn, gh, gn, woh, won)


def _mlp_kernel(h_ref, g_ref, wu_ref, wd_ref, gf_ref, o_ref, a_ref, acc_ref):
    i, j = pl.program_id(0), pl.program_id(1)
    last = pl.num_programs(1) - 1
    slot = i % 2

    def hidden_tile():
        u = jnp.dot(a_ref[slot], wu_ref[...], preferred_element_type=F32)
        u = jnp.square(jnp.maximum(u, 0.0)).astype(BF16)
        return jnp.dot(u, wd_ref[...], preferred_element_type=F32)

    @pl.when((i == 0) & (j == 0))
    def _():
        a_ref[0] = _rms(h_ref[...], g_ref[...]).astype(BF16)
        acc_ref[...] = h_ref[...] + hidden_tile()

    @pl.when((i > 0) & (j == 0))
    def _():
        o_ref[...] = _rms(acc_ref[...], gf_ref[...])
        acc_ref[...] = h_ref[...] + hidden_tile()

    @pl.when((j > 0) & (j < last))
    def _():
        acc_ref[...] += hidden_tile()

    @pl.when(j == last)
    def _():
        acc_ref[...] += hidden_tile()
        a_ref[1 - slot] = _rms(h_ref[...], g_ref[...]).astype(BF16)

    @pl.when((j == last) & (i == pl.num_programs(0) - 1))
    def _():
        o_ref[...] = _rms(acc_ref[...], gf_ref[...])


def _mlp(h, g, wu, wd, gf, tm, tf):
    ntok, d = h.shape
    dff = wu.shape[1]
    nt, nf = ntok // tm, dff // tf
    assert nf >= 2
    h_map = lambda i, j: (jnp.minimum(i + (j >= nf // 2).astype(jnp.int32), nt - 1), 0)
    return pl.pallas_call(
        _mlp_kernel,
        grid=(nt, nf),
        in_specs=[
            pl.BlockSpec((tm, d), h_map),
            pl.BlockSpec((1, d), lambda i, j: (0, 0)),
            pl.BlockSpec((d, tf), lambda i, j: (0, j)),
            pl.BlockSpec((tf, d), lambda i, j: (j, 0)),
            pl.BlockSpec((1, d), lambda i, j: (0, 0)),
        ],
        out_specs=pl.BlockSpec(
            (tm, d), lambda i, j: (jnp.maximum(i - (j == 0).astype(jnp.int32), 0), 0)),
        out_shape=jax.ShapeDtypeStruct((ntok, d), F32),
        scratch_shapes=[pltpu.VMEM((2, tm, d), BF16), pltpu.VMEM((tm, d), F32)],
        compiler_params=pltpu.CompilerParams(
            dimension_semantics=("arbitrary", "arbitrary"), vmem_limit_bytes=MLP_VMEM_LIMIT),
        name="mlp",
    )(h, g, wu, wd, gf)


def _positional_features(seq, emb):
    bands = (emb - 1) // 2
    t = np.linspace(0.0, 1.0, seq)
    w_ang = (2.0 * np.pi / seq) * np.arange(seq)
    f = np.linspace(1e-4, bands - 1, bands)
    ang = w_ang[None, :] * f[:, None]
    z = np.concatenate([t[None, :], np.cos(ang), -np.sin(ang)], axis=0)
    pad = (-z.shape[0]) % SUBLANES
    z = np.concatenate([z, np.zeros((pad, seq))], axis=0)
    return jnp.asarray(z, F32), jnp.asarray(t[None, :], F32)


def _lane_rep(v):
    return jnp.broadcast_to(v.astype(F32)[..., None], v.shape + (LANES,))


def _trunk(x, w):
    batch, seq, d = x.shape
    c = w["hy_bias"].shape[-1]
    r = seq // LANES
    rows = seq // GRID_W
    assert seq % LANES == 0 and c % STRIP == 0
    assert rows >= NAT_KH and seq % GRID_W == 0
    consts = _fft_constants(r)
    cs = HYENA_CS

    emb = w["hy_pe_w1"].shape[0]
    zT, tvec = _positional_features(seq, emb)
    w1T = jnp.pad(w["hy_pe_w1"].astype(F32).T, ((0, 0), (0, zT.shape[0] - emb)))
    col = lambda v: v.astype(F32)[:, None]
    max_decay = math.log(HY_DECAY_TARGET) / HY_FAST_DECAY
    min_decay = math.log(HY_DECAY_TARGET) / HY_SLOW_DECAY
    absd = jnp.asarray(np.abs(np.linspace(min_decay, max_decay, c)), F32)
    w4T = w["hy_pe_w4"].astype(F32).T
    hi_bits = lax.bitcast_convert_type(w4T, jnp.uint32) & jnp.uint32(0xFFFF0000)
    w4_hi_f32 = lax.bitcast_convert_type(hi_bits, F32)
    w4_hi = w4_hi_f32.astype(BF16)
    w4_lo = (w4T - w4_hi_f32).astype(BF16)
    hT = _filter_mlp(zT, tvec, w1T, col(w["hy_pe_b1"]), w["hy_pe_w2"].astype(F32).T,
                     col(w["hy_pe_b2"]), w["hy_pe_w3"].astype(F32).T, col(w["hy_pe_b3"]),
                     col(w["hy_pe_freq"]), jnp.concatenate([w4_hi, w4_hi, w4_lo], axis=1),
                     _lane_rep(absd), tl=FILTER_TL)
    khat = _filter_fft(hT, consts, c, r, cs)

    x2 = x.reshape(batch * seq, d)
    w_in = w["w_in"]
    n_hy = 3 * c
    whT = w_in[:, :n_hy].T.astype(BF16)
    wn = w_in[:, n_hy:].astype(BF16)
    pT, pn = _in_proj(x2, w["norm_mix_g"].astype(F32)[None, :], whT, wn, batch, seq,
                      tm=PROJ_TM)

    cw = w["hy_conv_w"].astype(F32).reshape(3, 3, c)
    cb = w["hy_conv_b"].astype(F32).reshape(3, c)
    par = [jnp.concatenate([cw[:, p], cb[p][None]], axis=0) for p in range(3)]
    par = _lane_rep(jnp.concatenate(par + [w["hy_bias"].astype(F32)], axis=0))
    yhT = _hyena(pT, par, khat, consts, batch, c, cs, strips_per_iter=4 if r <= 32 else 2)

    nat_w = pn.shape[1] // 3
    bias = _nat_bias_table(w["nat_rpb"])
    yn = _nat(pn.reshape(batch, seq, 3 * nat_w), bias, batch, seq, nat_w, nat_w // NAT_HEADS)
    yn = yn.reshape(batch * seq, nat_w)

    w_out = w["w_out"].astype(BF16)
    h = _out_proj(x2, yhT, yn, _lane_rep(w["gnorm_hy"]), w["gnorm_nat"].astype(F32)[None, :],
                  w_out[:c], w_out[c:], seq, tm=PROJ_TM)
    y = _mlp(h, w["norm_mlp_g"].astype(F32)[None, :], w["w_up"].astype(BF16),
             w["w_down"].astype(BF16), w["norm_f_g"].astype(F32)[None, :],
             tm=PROJ_TM, tf=MLP_TF)
    return y.reshape(batch, seq, d)


def kernel(x_prompt, x_sample, norm_mix_g, w_in, hy_conv_w, hy_conv_b, hy_pe_w1, hy_pe_b1,
           hy_pe_w2, hy_pe_b2, hy_pe_w3, hy_pe_b3, hy_pe_freq, hy_pe_w4, hy_bias, nat_rpb,
           gnorm_hy, gnorm_nat, w_out, norm_mlp_g, w_up, w_down, norm_f_g):
    assert norm_mix_g.shape[0] == 1, "single-layer trunk"
    w = dict(norm_mix_g=norm_mix_g[0], w_in=w_in[0], hy_conv_w=hy_conv_w[0],
             hy_conv_b=hy_conv_b[0], hy_pe_w1=hy_pe_w1[0], hy_pe_b1=hy_pe_b1[0],
             hy_pe_w2=hy_pe_w2[0], hy_pe_b2=hy_pe_b2[0], hy_pe_w3=hy_pe_w3[0],
             hy_pe_b3=hy_pe_b3[0], hy_pe_freq=hy_pe_freq[0], hy_pe_w4=hy_pe_w4[0],
             hy_bias=hy_bias[0], nat_rpb=nat_rpb[0], gnorm_hy=gnorm_hy[0],
             gnorm_nat=gnorm_nat[0], w_out=w_out[0], norm_mlp_g=norm_mlp_g[0],
             w_up=w_up[0], w_down=w_down[0], norm_f_g=norm_f_g)
    return (_trunk(x_prompt, w), _trunk(x_sample, w))
```
